```python
import math
import jax
import jax.numpy as jnp
from jax import lax
import numpy as np

D_MODEL = 1024
BATCH = 8
SEQ = 2048
DEPTH = 1
DEC_BATCH = 128
DEC_SEQ = 8
PAST_LEN = 8192
PAGE_SIZE = 128

HEAD_DIM = 64
N_HEADS = D_MODEL // HEAD_DIM
H_DSA = N_HEADS // 2
H_FOX = N_HEADS - H_DSA
W_DSA = H_DSA * HEAD_DIM
W_FOX = H_FOX * HEAD_DIM
MIX_WIDTH = W_DSA + W_FOX
H_IDX = 8
D_IDX = 64
TOPK_MAX = 256
Q_BLOCK = 128
ROPE_THETA = 10000.0
N_EXPERTS = 256
TOP_K = 8
N_GROUPS = 8
TOPK_GROUPS = 4
D_EXPERT = 256
D_SHARED = 256
ROUTED_SCALE = 2.5
EXPERT_BLOCK = 128
LN_EPS = 1e-5
FORGET_BIAS_INIT = 3.0
DEEPNORM_ALPHA = (2 * DEPTH) ** 0.25
DEEPNORM_BETA = (8 * DEPTH) ** -0.25

_IN_SIZES = (W_DSA, W_DSA, W_DSA, H_IDX * D_IDX, D_IDX, H_IDX, W_FOX, W_FOX, W_FOX, H_FOX)
IN_COLS = sum(_IN_SIZES)
SPLIT_POINTS = tuple(int(v) for v in np.cumsum(_IN_SIZES)[:-1])

kernel_name = 'hybrid_dsa_fox_moe_deepnorm_step'


def layer_norm(x, g, b):
    xf = x.astype(jnp.float32)
    mu = xf.mean(-1, keepdims=True)
    var = jnp.square(xf - mu).mean(-1, keepdims=True)
    return ((xf - mu) * lax.rsqrt(var + LN_EPS) * g.astype(jnp.float32) + b.astype(jnp.float32)).astype(x.dtype)


def rope(x, pos):
    half = x.shape[-1] // 2
    freqs = ROPE_THETA ** (-jnp.arange(half, dtype=jnp.float32) / half)
    ang = pos.astype(jnp.float32)[:, None] * freqs[None, :]
    cos = jnp.cos(ang)[None, :, None, :]
    sin = jnp.sin(ang)[None, :, None, :]
    xf = x.astype(jnp.float32)
    x1, x2 = xf[..., :half], xf[..., half:]
    return jnp.concatenate([x1 * cos - x2 * sin, x2 * cos + x1 * sin], axis=-1).astype(x.dtype)


def project_mixers(x, pos, w_in, b_forget, idx_ln_g, idx_ln_b):
    B, T, _ = x.shape
    h = jnp.einsum('btd,dc->btc', x, w_in)
    q_a, k_a, v_a, q_i, k_i, w_i, q_f, k_f, v_f, f_logit = jnp.split(h, SPLIT_POINTS, axis=-1)
    q_a = rope(q_a.reshape(B, T, H_DSA, HEAD_DIM), pos)
    k_a = rope(k_a.reshape(B, T, H_DSA, HEAD_DIM), pos)
    v_a = v_a.reshape(B, T, H_DSA, HEAD_DIM)
    q_i = rope(q_i.reshape(B, T, H_IDX, D_IDX), pos)
    k_i = rope(layer_norm(k_i, idx_ln_g, idx_ln_b)[:, :, None, :], pos)[:, :, 0, :]
    w_i = w_i.astype(jnp.float32) * (H_IDX ** -0.5 * D_IDX ** -0.5)
    q_f = q_f.reshape(B, T, H_FOX, HEAD_DIM)
    k_f = k_f.reshape(B, T, H_FOX, HEAD_DIM)
    v_f = v_f.reshape(B, T, H_FOX, HEAD_DIM)
    logf = jax.nn.log_sigmoid((f_logit + b_forget).astype(jnp.float32))
    return q_a, k_a, v_a, q_i, k_i, w_i, q_f, k_f, v_f, logf


def indexer_scores(q_i, k_i, w_i):
    s = jnp.einsum('bqhd,bld->bqhl', q_i, k_i).astype(jnp.float32)
    return jnp.einsum('bqhl,bqh->bql', jax.nn.relu(s), w_i)


def select_topk(scores, mask, k):
    masked = jnp.where(mask, scores, -jnp.inf)
    vals, idx = lax.top_k(masked, k)
    return idx, vals > -jnp.inf


def gather_rows(src, idx):
    return jax.vmap(lambda s, i: s[i])(src, idx)


def sparse_attend(q, kg, vg, valid):
    s = jnp.einsum('bqhd,bqkhd->bqhk', q, kg).astype(jnp.float32) * HEAD_DIM ** -0.5
    s = jnp.where(valid[:, :, None, :], s, -jnp.inf)
    p = jax.nn.softmax(s, axis=-1)
    return jnp.einsum('bqhk,bqkhd->bqhd', p, vg)


def dsa_prompt(q, k, v, q_i, k_i, w_i):
    B, S = q.shape[:2]
    topk = min(TOPK_MAX, S // 4)
    kpos = jnp.arange(S)

    def block(i):
        q0 = i * Q_BLOCK
        qb = lax.dynamic_slice_in_dim(q, q0, Q_BLOCK, axis=1)
        qib = lax.dynamic_slice_in_dim(q_i, q0, Q_BLOCK, axis=1)
        wib = lax.dynamic_slice_in_dim(w_i, q0, Q_BLOCK, axis=1)
        qpos = q0 + jnp.arange(Q_BLOCK)
        scores = indexer_scores(qib, k_i, wib)
        idx, valid = select_topk(scores, kpos[None, :] <= qpos[:, None], topk)
        return sparse_attend(qb, gather_rows(k, idx), gather_rows(v, idx), valid)

    o = lax.map(block, jnp.arange(S // Q_BLOCK))
    return o.transpose(1, 0, 2, 3, 4).reshape(B, S, H_DSA, HEAD_DIM)


def dsa_sample(q, k_new, v_new, q_i, k_i_new, w_i, cache_k, cache_v, cache_ki, page_table):
    B, T = q.shape[:2]
    n_pages = page_table.shape[1]
    P = n_pages * PAGE_SIZE
    L = P + T
    topk = min(TOPK_MAX, L // 4)
    ki_past = cache_ki[page_table].reshape(B, P, D_IDX)
    ki_all = jnp.concatenate([ki_past, k_i_new.astype(ki_past.dtype)], axis=1)
    scores = indexer_scores(q_i, ki_all, w_i)
    qpos = P + jnp.arange(T)
    kpos = jnp.arange(L)
    idx, valid = select_topk(scores, kpos[None, :] <= qpos[:, None], topk)
    is_past = (idx < P)[..., None, None]
    pidx = jnp.minimum(idx, P - 1)
    phys = jax.vmap(lambda pt, i: pt[i])(page_table, pidx // PAGE_SIZE)
    slot = pidx % PAGE_SIZE
    nidx = jnp.clip(idx - P, 0, T - 1)
    kg = jnp.where(is_past, cache_k[phys, slot], gather_rows(k_new.astype(cache_k.dtype), nidx))
    vg = jnp.where(is_past, cache_v[phys, slot], gather_rows(v_new.astype(cache_v.dtype), nidx))
    return sparse_attend(q, kg, vg, valid)


def fox_prompt(q, k, v, logf):
    B, S = q.shape[:2]
    c = jnp.cumsum(logf, axis=1).transpose(0, 2, 1)
    kpos = jnp.arange(S)

    def block(i):
        q0 = i * Q_BLOCK
        qb = lax.dynamic_slice_in_dim(q, q0, Q_BLOCK, axis=1)
        cb = lax.dynamic_slice_in_dim(c, q0, Q_BLOCK, axis=2)
        qpos = q0 + jnp.arange(Q_BLOCK)
        s = (jnp.einsum('bqhd,bshd->bhqs', qb, k).astype(jnp.float32) * HEAD_DIM ** -0.5
             + cb[:, :, :, None] - c[:, :, None, :])
        s = jnp.where(kpos[None, :] <= qpos[:, None], s, -jnp.inf)
        p = jax.nn.softmax(s, axis=-1)
        return jnp.einsum('bhqs,bshd->bqhd', p, v)

    o = lax.map(block, jnp.arange(S // Q_BLOCK))
    return o.transpose(1, 0, 2, 3, 4).reshape(B, S, H_FOX, HEAD_DIM)


def fox_sample(q, k_new, v_new, logf_new, cache_k, cache_v, cache_logf, page_table):
    B, T = q.shape[:2]
    n_pages = page_table.shape[1]
    P = n_pages * PAGE_SIZE
    scale = HEAD_DIM ** -0.5
    logf_past = cache_logf[page_table].reshape(B, P, H_FOX).astype(jnp.float32)
    r = (jnp.cumsum(logf_past[:, ::-1], axis=1)[:, ::-1] - logf_past).transpose(0, 2, 1)
    cn = jnp.cumsum(logf_new, axis=1).transpose(0, 2, 1)
    s_new = (jnp.einsum('bthd,bshd->bhts', q, k_new).astype(jnp.float32) * scale
             + cn[:, :, :, None] - cn[:, :, None, :])
    tpos = jnp.arange(T)
    s_new = jnp.where(tpos[None, :] <= tpos[:, None], s_new, -jnp.inf)
    m_new = s_new.max(-1)
    e_new = jnp.exp(s_new - m_new[..., None])
    l_new = e_new.sum(-1)
    acc_new = jnp.einsum('bhts,bshd->bhtd', e_new, v_new)

    def page(p):
        phys = page_table[:, p]
        kp = cache_k[phys]
        vp = cache_v[phys]
        rp = lax.dynamic_slice_in_dim(r, p * PAGE_SIZE, PAGE_SIZE, axis=2)
        s = (jnp.einsum('bthd,bshd->bhts', q, kp).astype(jnp.float32) * scale
             + cn[:, :, :, None] + rp[:, :, None, :])
        m = s.max(-1)
        e = jnp.exp(s - m[..., None])
        return m, e.sum(-1), jnp.einsum('bhts,bshd->bhtd', e, vp)

    m_p, l_p, acc_p = lax.map(page, jnp.arange(n_pages))
    m = jnp.maximum(m_p.max(0), m_new)
    sc_p = jnp.exp(m_p - m[None])
    sc_n = jnp.exp(m_new - m)
    l = (l_p * sc_p).sum(0) + l_new * sc_n
    acc = (acc_p * sc_p[..., None]).sum(0) + acc_new * sc_n[..., None]
    return (acc / l[..., None]).transpose(0, 2, 1, 3)


def swiglu(x, wg, wu, wd):
    return (jax.nn.silu(x @ wg) * (x @ wu)) @ wd


def route(x2, w_router, router_bias):
    n = x2.shape[0]
    scores = jax.nn.sigmoid(jnp.einsum('nd,de->ne', x2, w_router).astype(jnp.float32))
    biased = scores + router_bias.astype(jnp.float32)
    grp = biased.reshape(n, N_GROUPS, N_EXPERTS // N_GROUPS)
    grp_score = lax.top_k(grp, 2)[0].sum(-1)
    _, top_g = lax.top_k(grp_score, TOPK_GROUPS)
    keep_g = (top_g[:, :, None] == jnp.arange(N_GROUPS)[None, None, :]).any(1)
    masked = jnp.where(keep_g[:, :, None], grp, -jnp.inf).reshape(n, N_EXPERTS)
    _, idx = lax.top_k(masked, TOP_K)
    w = jnp.take_along_axis(scores, idx, axis=-1)
    w = w / w.sum(-1, keepdims=True) * ROUTED_SCALE
    return idx, w


def routed_experts(x2, idx, gate, w_gate_e, w_up_e, w_down_e):
    n, d = x2.shape
    a = n * TOP_K
    n_blocks = (a + N_EXPERTS * (EXPERT_BLOCK - 1) + EXPERT_BLOCK - 1) // EXPERT_BLOCK
    rows = n_blocks * EXPERT_BLOCK
    flat_e = idx.reshape(-1)
    flat_g = gate.reshape(-1)
    order = jnp.argsort(flat_e)
    sorted_e = flat_e[order]
    counts = jnp.bincount(flat_e, length=N_EXPERTS)
    starts = jnp.cumsum(counts) - counts
    padded = (counts + EXPERT_BLOCK - 1) // EXPERT_BLOCK * EXPERT_BLOCK
    pad_end = jnp.cumsum(padded)
    pad_start = pad_end - padded
    dest = pad_start[sorted_e] + jnp.arange(a) - starts[sorted_e]
    row_token = jnp.full((rows,), n, jnp.int32).at[dest].set((order // TOP_K).astype(jnp.int32))
    row_gate = jnp.zeros((rows,), jnp.float32).at[dest].set(flat_g[order])
    block_expert = jnp.minimum(
        jnp.searchsorted(pad_end, jnp.arange(n_blocks) * EXPERT_BLOCK, side='right'), N_EXPERTS - 1)
    x_pad = jnp.concatenate([x2, jnp.zeros((1, d), x2.dtype)], axis=0)
    xb = x_pad[row_token].reshape(n_blocks, EXPERT_BLOCK, d)

    def run(args):
        xblk, e = args
        return swiglu(xblk, w_gate_e[e], w_up_e[e], w_down_e[e])

    yb = lax.map(run, (xb, block_expert)).reshape(rows, d)
    return jax.ops.segment_sum(yb * row_gate[:, None], row_token, num_segments=n + 1)[:n]


def moe_ffn(x, w_router, router_bias, w_gate_e, w_up_e, w_down_e, w_gate_s, w_up_s, w_down_s):
    B, T, d = x.shape
    x2 = x.reshape(B * T, d)
    idx, gate = route(x2, w_router, router_bias)
    y = routed_experts(x2, idx, gate, w_gate_e, w_up_e, w_down_e) + swiglu(x2, w_gate_s, w_up_s, w_down_s)
    return y.reshape(B, T, d).astype(x.dtype)


def finish_layer(x, o_a, o_f, w_o, ln1_g, ln1_b, w_router, router_bias,
                 w_gate_e, w_up_e, w_down_e, w_gate_s, w_up_s, w_down_s, ln2_g, ln2_b):
    B, T, _ = x.shape
    heads = jnp.concatenate([o_a.reshape(B, T, W_DSA), o_f.reshape(B, T, W_FOX)], axis=-1).astype(x.dtype)
    mix = jnp.einsum('btc,cd->btd', heads, w_o)
    h = layer_norm(DEEPNORM_ALPHA * x + mix, ln1_g, ln1_b)
    f = moe_ffn(h, w_router, router_bias, w_gate_e, w_up_e, w_down_e, w_gate_s, w_up_s, w_down_s)
    return layer_norm(DEEPNORM_ALPHA * h + f, ln2_g, ln2_b)


def setup_inputs(seed: int = 0) -> dict:
    key = jax.random.key(seed)
    ks = jax.random.split(key, 32)
    f32 = jnp.float32
    n_pages = PAST_LEN // PAGE_SIZE
    n_used = DEC_BATCH * n_pages
    n_pool = n_used + n_used // 4

    def nrm(k, shape, scale):
        return jax.random.normal(k, shape, f32) * scale

    return {
        'x_prompt': nrm(ks[0], (BATCH, SEQ, D_MODEL), 1.0),
        'x_sample': nrm(ks[1], (DEC_BATCH, DEC_SEQ, D_MODEL), 1.0),
        'cache_k_dsa': nrm(ks[2], (DEPTH, n_pool, PAGE_SIZE, H_DSA, HEAD_DIM), 1.0),
        'cache_v_dsa': nrm(ks[3], (DEPTH, n_pool, PAGE_SIZE, H_DSA, HEAD_DIM), 1.0),
        'cache_k_idx': nrm(ks[4], (DEPTH, n_pool, PAGE_SIZE, D_IDX), 1.0),
        'cache_k_fox': nrm(ks[5], (DEPTH, n_pool, PAGE_SIZE, H_FOX, HEAD_DIM), 1.0),
        'cache_v_fox': nrm(ks[6], (DEPTH, n_pool, PAGE_SIZE, H_FOX, HEAD_DIM), 1.0),
        'cache_logf_fox': jax.nn.log_sigmoid(FORGET_BIAS_INIT + nrm(ks[7], (DEPTH, n_pool, PAGE_SIZE, H_FOX), 1.0)),
        'page_table': jax.random.permutation(ks[8], n_pool)[:n_used].reshape(DEC_BATCH, n_pages).astype(jnp.int32),
        'w_in': nrm(ks[9], (DEPTH, D_MODEL, IN_COLS), D_MODEL ** -0.5),
        'b_forget': FORGET_BIAS_INIT + nrm(ks[10], (DEPTH, H_FOX), 0.1),
        'idx_ln_g': 1.0 + nrm(ks[11], (DEPTH, D_IDX), 0.02),
        'idx_ln_b': nrm(ks[12], (DEPTH, D_IDX), 0.02),
        'w_o': nrm(ks[13], (DEPTH, MIX_WIDTH, D_MODEL), MIX_WIDTH ** -0.5 * DEEPNORM_BETA),
        'ln1_g': 1.0 + nrm(ks[14], (DEPTH, D_MODEL), 0.02),
        'ln1_b': nrm(ks[15], (DEPTH, D_MODEL), 0.02),
        'w_router': nrm(ks[16], (DEPTH, D_MODEL, N_EXPERTS), D_MODEL ** -0.5),
        'router_bias': nrm(ks[17], (DEPTH, N_EXPERTS), 0.01),
        'w_gate_e': nrm(ks[18], (DEPTH, N_EXPERTS, D_MODEL, D_EXPERT), D_MODEL ** -0.5),
        'w_up_e': nrm(ks[19], (DEPTH, N_EXPERTS, D_MODEL, D_EXPERT), D_MODEL ** -0.5),
        'w_down_e': nrm(ks[20], (DEPTH, N_EXPERTS, D_EXPERT, D_MODEL), D_EXPERT ** -0.5 * DEEPNORM_BETA),
        'w_gate_s': nrm(ks[21], (DEPTH, D_MODEL, D_SHARED), D_MODEL ** -0.5),
        'w_up_s': nrm(ks[22], (DEPTH, D_MODEL, D_SHARED), D_MODEL ** -0.5),
        'w_down_s': nrm(ks[23], (DEPTH, D_SHARED, D_MODEL), D_SHARED ** -0.5 * DEEPNORM_BETA),
        'ln2_g': 1.0 + nrm(ks[24], (DEPTH, D_MODEL), 0.02),
        'ln2_b': nrm(ks[25], (DEPTH, D_MODEL), 0.02),
    }


def reference(x_prompt, x_sample, cache_k_dsa, cache_v_dsa, cache_k_idx, cache_k_fox, cache_v_fox,
              cache_logf_fox, page_table, w_in, b_forget, idx_ln_g, idx_ln_b, w_o, ln1_g, ln1_b,
              w_router, router_bias, w_gate_e, w_up_e, w_down_e, w_gate_s, w_up_s, w_down_s,
              ln2_g, ln2_b):
    seq = x_prompt.shape[1]
    dec_seq = x_sample.shape[1]
    past = page_table.shape[1] * PAGE_SIZE
    pos_p = jnp.arange(seq, dtype=jnp.int32)
    pos_s = past + jnp.arange(dec_seq, dtype=jnp.int32)
    xp, xs = x_prompt, x_sample
    kdp, vdp, kip, kfp, vfp, lfp = [], [], [], [], [], []
    kds, vds, kis, kfs, vfs, lfs = [], [], [], [], [], []
    for l in range(DEPTH):
        ffn = (w_o[l], ln1_g[l], ln1_b[l], w_router[l], router_bias[l], w_gate_e[l], w_up_e[l],
               w_down_e[l], w_gate_s[l], w_up_s[l], w_down_s[l], ln2_g[l], ln2_b[l])
        q_a, k_a, v_a, q_i, k_i, w_i, q_f, k_f, v_f, logf = project_mixers(
            xp, pos_p, w_in[l], b_forget[l], idx_ln_g[l], idx_ln_b[l])
        o_a = dsa_prompt(q_a, k_a, v_a, q_i, k_i, w_i)
        o_f = fox_prompt(q_f, k_f, v_f, logf)
        xp = finish_layer(xp, o_a, o_f, *ffn)
        kdp.append(k_a); vdp.append(v_a); kip.append(k_i)
        kfp.append(k_f); vfp.append(v_f); lfp.append(logf.astype(x_prompt.dtype))
        q_a, k_a, v_a, q_i, k_i, w_i, q_f, k_f, v_f, logf = project_mixers(
            xs, pos_s, w_in[l], b_forget[l], idx_ln_g[l], idx_ln_b[l])
        o_a = dsa_sample(q_a, k_a, v_a, q_i, k_i, w_i, cache_k_dsa[l], cache_v_dsa[l], cache_k_idx[l], page_table)
        o_f = fox_sample(q_f, k_f, v_f, logf, cache_k_fox[l], cache_v_fox[l], cache_logf_fox[l], page_table)
        xs = finish_layer(xs, o_a, o_f, *ffn)
        kds.append(k_a); vds.append(v_a); kis.append(k_i)
        kfs.append(k_f); vfs.append(v_f); lfs.append(logf.astype(x_sample.dtype))
    return (xp, xs,
            jnp.stack(kdp), jnp.stack(vdp), jnp.stack(kip), jnp.stack(kfp), jnp.stack(vfp), jnp.stack(lfp),
            jnp.stack(kds), jnp.stack(vds), jnp.stack(kis), jnp.stack(kfs), jnp.stack(vfs), jnp.stack(lfs))
```

```python
import functools
import math
import jax
import jax.numpy as jnp
from jax import lax
import numpy as np
from jax.experimental import pallas as pl
from jax.experimental.pallas import tpu as pltpu

D_MODEL = 1024
PAGE_SIZE = 128
HEAD_DIM = 64
N_HEADS = D_MODEL // HEAD_DIM
H_DSA = N_HEADS // 2
H_FOX = N_HEADS - H_DSA
W_DSA = H_DSA * HEAD_DIM
W_FOX = H_FOX * HEAD_DIM
MIX_WIDTH = W_DSA + W_FOX
H_IDX = 8
D_IDX = 64
TOPK_MAX = 256
Q_BLOCK = 128
ROPE_THETA = 10000.0
N_EXPERTS = 256
TOP_K = 8
N_GROUPS = 8
TOPK_GROUPS = 4
D_EXPERT = 256
D_SHARED = 256
ROUTED_SCALE = 2.5
EXPERT_BLOCK = 128
LN_EPS = 1e-5
DEPTH = 1
DEEPNORM_ALPHA = (2 * DEPTH) ** 0.25

_IN_SIZES = (W_DSA, W_DSA, W_DSA, H_IDX * D_IDX, D_IDX, H_IDX, W_FOX, W_FOX, W_FOX, H_FOX)
IN_COLS = sum(_IN_SIZES)
SPLIT_POINTS = tuple(int(v) for v in np.cumsum(_IN_SIZES)[:-1])

LANES = 128


def _proj_body(x_ref, w_ref, o_ref):
    o_ref[...] = jnp.dot(x_ref[...].astype(jnp.bfloat16), w_ref[...],
                         preferred_element_type=jnp.float32)


def _project(x2, w):
    m, d = x2.shape
    c = w.shape[1]
    cp = (c + LANES - 1) // LANES * LANES
    wp = jnp.pad(w, ((0, 0), (0, cp - c))).astype(jnp.bfloat16)
    tm = 256
    out = pl.pallas_call(
        _proj_body,
        grid=(m // tm,),
        in_specs=[pl.BlockSpec((tm, d), lambda i: (i, 0)),
                  pl.BlockSpec((d, cp), lambda i: (0, 0))],
        out_specs=pl.BlockSpec((tm, cp), lambda i: (i, 0)),
        out_shape=jax.ShapeDtypeStruct((m, cp), jnp.float32),
        compiler_params=pltpu.CompilerParams(vmem_limit_bytes=48 * 1024 * 1024),
        name="in_proj",
    )(x2, wp)
    return out[:, :c]


def layer_norm(x, g, b):
    xf = x.astype(jnp.float32)
    mu = xf.mean(-1, keepdims=True)
    var = jnp.square(xf - mu).mean(-1, keepdims=True)
    return ((xf - mu) * lax.rsqrt(var + LN_EPS) * g.astype(jnp.float32) + b.astype(jnp.float32)).astype(x.dtype)


def rope(x, pos):
    half = x.shape[-1] // 2
    freqs = ROPE_THETA ** (-jnp.arange(half, dtype=jnp.float32) / half)
    ang = pos.astype(jnp.float32)[:, None] * freqs[None, :]
    cos = jnp.cos(ang)[None, :, None, :]
    sin = jnp.sin(ang)[None, :, None, :]
    xf = x.astype(jnp.float32)
    x1, x2 = xf[..., :half], xf[..., half:]
    return jnp.concatenate([x1 * cos - x2 * sin, x2 * cos + x1 * sin], axis=-1).astype(x.dtype)


def project_mixers(x, pos, w_in, b_forget, idx_ln_g, idx_ln_b):
    B, T, _ = x.shape
    h = _project(x.reshape(B * T, D_MODEL), w_in).reshape(B, T, IN_COLS)
    q_a, k_a, v_a, q_i, k_i, w_i, q_f, k_f, v_f, f_logit = jnp.split(h, SPLIT_POINTS, axis=-1)
    q_a = rope(q_a.reshape(B, T, H_DSA, HEAD_DIM), pos)
    k_a = rope(k_a.reshape(B, T, H_DSA, HEAD_DIM), pos)
    v_a = v_a.reshape(B, T, H_DSA, HEAD_DIM)
    q_i = rope(q_i.reshape(B, T, H_IDX, D_IDX), pos)
    k_i = rope(layer_norm(k_i, idx_ln_g, idx_ln_b)[:, :, None, :], pos)[:, :, 0, :]
    w_i = w_i.astype(jnp.float32) * (H_IDX ** -0.5 * D_IDX ** -0.5)
    q_f = q_f.reshape(B, T, H_FOX, HEAD_DIM)
    k_f = k_f.reshape(B, T, H_FOX, HEAD_DIM)
    v_f = v_f.reshape(B, T, H_FOX, HEAD_DIM)
    logf = jax.nn.log_sigmoid((f_logit + b_forget).astype(jnp.float32))
    return q_a, k_a, v_a, q_i, k_i, w_i, q_f, k_f, v_f, logf


def indexer_scores(q_i, k_i, w_i):
    s = jnp.einsum('bqhd,bld->bqhl', q_i, k_i).astype(jnp.float32)
    return jnp.einsum('bqhl,bqh->bql', jax.nn.relu(s), w_i)


def select_topk(scores, mask, k):
    masked = jnp.where(mask, scores, -jnp.inf)
    vals, idx = lax.top_k(masked, k)
    return idx, vals > -jnp.inf


def gather_rows(src, idx):
    return jax.vmap(lambda s, i: s[i])(src, idx)


def sparse_attend(q, kg, vg, valid):
    s = jnp.einsum('bqhd,bqkhd->bqhk', q, kg).astype(jnp.float32) * HEAD_DIM ** -0.5
    s = jnp.where(valid[:, :, None, :], s, -jnp.inf)
    p = jax.nn.softmax(s, axis=-1)
    return jnp.einsum('bqhk,bqkhd->bqhd', p, vg)


def dsa_prompt(q, k, v, q_i, k_i, w_i):
    B, S = q.shape[:2]
    topk = min(TOPK_MAX, S // 4)
    kpos = jnp.arange(S)

    def block(i):
        q0 = i * Q_BLOCK
        qb = lax.dynamic_slice_in_dim(q, q0, Q_BLOCK, axis=1)
        qib = lax.dynamic_slice_in_dim(q_i, q0, Q_BLOCK, axis=1)
        wib = lax.dynamic_slice_in_dim(w_i, q0, Q_BLOCK, axis=1)
        qpos = q0 + jnp.arange(Q_BLOCK)
        scores = indexer_scores(qib, k_i, wib)
        idx, valid = select_topk(scores, kpos[None, :] <= qpos[:, None], topk)
        return sparse_attend(qb, gather_rows(k, idx), gather_rows(v, idx), valid)

    o = lax.map(block, jnp.arange(S // Q_BLOCK))
    return o.transpose(1, 0, 2, 3, 4).reshape(B, S, H_DSA, HEAD_DIM)


def dsa_sample(q, k_new, v_new, q_i, k_i_new, w_i, cache_k, cache_v, cache_ki, page_table):
    B, T = q.shape[:2]
    n_pages = page_table.shape[1]
    P = n_pages * PAGE_SIZE
    L = P + T
    topk = min(TOPK_MAX, L // 4)
    ki_past = cache_ki[page_table].reshape(B, P, D_IDX)
    ki_all = jnp.concatenate([ki_past, k_i_new.astype(ki_past.dtype)], axis=1)
    scores = indexer_scores(q_i, ki_all, w_i)
    qpos = P + jnp.arange(T)
    kpos = jnp.arange(L)
    idx, valid = select_topk(scores, kpos[None, :] <= qpos[:, None], topk)
    is_past = (idx < P)[..., None, None]
    pidx = jnp.minimum(idx, P - 1)
    phys = jax.vmap(lambda pt, i: pt[i])(page_table, pidx // PAGE_SIZE)
    slot = pidx % PAGE_SIZE
    nidx = jnp.clip(idx - P, 0, T - 1)
    kg = jnp.where(is_past, cache_k[phys, slot], gather_rows(k_new.astype(cache_k.dtype), nidx))
    vg = jnp.where(is_past, cache_v[phys, slot], gather_rows(v_new.astype(cache_v.dtype), nidx))
    return sparse_attend(q, kg, vg, valid)


def fox_prompt(q, k, v, logf):
    B, S = q.shape[:2]
    c = jnp.cumsum(logf, axis=1).transpose(0, 2, 1)
    kpos = jnp.arange(S)

    def block(i):
        q0 = i * Q_BLOCK
        qb = lax.dynamic_slice_in_dim(q, q0, Q_BLOCK, axis=1)
        cb = lax.dynamic_slice_in_dim(c, q0, Q_BLOCK, axis=2)
        qpos = q0 + jnp.arange(Q_BLOCK)
        s = (jnp.einsum('bqhd,bshd->bhqs', qb, k).astype(jnp.float32) * HEAD_DIM ** -0.5
             + cb[:, :, :, None] - c[:, :, None, :])
        s = jnp.where(kpos[None, :] <= qpos[:, None], s, -jnp.inf)
        p = jax.nn.softmax(s, axis=-1)
        return jnp.einsum('bhqs,bshd->bqhd', p, v)

    o = lax.map(block, jnp.arange(S // Q_BLOCK))
    return o.transpose(1, 0, 2, 3, 4).reshape(B, S, H_FOX, HEAD_DIM)


def fox_sample(q, k_new, v_new, logf_new, cache_k, cache_v, cache_logf, page_table):
    B, T = q.shape[:2]
    n_pages = page_table.shape[1]
    P = n_pages * PAGE_SIZE
    scale = HEAD_DIM ** -0.5
    logf_past = cache_logf[page_table].reshape(B, P, H_FOX).astype(jnp.float32)
    r = (jnp.cumsum(logf_past[:, ::-1], axis=1)[:, ::-1] - logf_past).transpose(0, 2, 1)
    cn = jnp.cumsum(logf_new, axis=1).transpose(0, 2, 1)
    s_new = (jnp.einsum('bthd,bshd->bhts', q, k_new).astype(jnp.float32) * scale
             + cn[:, :, :, None] - cn[:, :, None, :])
    tpos = jnp.arange(T)
    s_new = jnp.where(tpos[None, :] <= tpos[:, None], s_new, -jnp.inf)
    m_new = s_new.max(-1)
    e_new = jnp.exp(s_new - m_new[..., None])
    l_new = e_new.sum(-1)
    acc_new = jnp.einsum('bhts,bshd->bhtd', e_new, v_new)

    def page(p):
        phys = page_table[:, p]
        kp = cache_k[phys]
        vp = cache_v[phys]
        rp = lax.dynamic_slice_in_dim(r, p * PAGE_SIZE, PAGE_SIZE, axis=2)
        s = (jnp.einsum('bthd,bshd->bhts', q, kp).astype(jnp.float32) * scale
             + cn[:, :, :, None] + rp[:, :, None, :])
        m = s.max(-1)
        e = jnp.exp(s - m[..., None])
        return m, e.sum(-1), jnp.einsum('bhts,bshd->bhtd', e, vp)

    m_p, l_p, acc_p = lax.map(page, jnp.arange(n_pages))
    m = jnp.maximum(m_p.max(0), m_new)
    sc_p = jnp.exp(m_p - m[None])
    sc_n = jnp.exp(m_new - m)
    l = (l_p * sc_p).sum(0) + l_new * sc_n
    acc = (acc_p * sc_p[..., None]).sum(0) + acc_new * sc_n[..., None]
    return (acc / l[..., None]).transpose(0, 2, 1, 3)


def swiglu(x, wg, wu, wd):
    return (jax.nn.silu(x @ wg) * (x @ wu)) @ wd


def route(x2, w_router, router_bias):
    n = x2.shape[0]
    scores = jax.nn.sigmoid(jnp.einsum('nd,de->ne', x2, w_router).astype(jnp.float32))
    biased = scores + router_bias.astype(jnp.float32)
    grp = biased.reshape(n, N_GROUPS, N_EXPERTS // N_GROUPS)
    grp_score = lax.top_k(grp, 2)[0].sum(-1)
    _, top_g = lax.top_k(grp_score, TOPK_GROUPS)
    keep_g = (top_g[:, :, None] == jnp.arange(N_GROUPS)[None, None, :]).any(1)
    masked = jnp.where(keep_g[:, :, None], grp, -jnp.inf).reshape(n, N_EXPERTS)
    _, idx = lax.top_k(masked, TOP_K)
    w = jnp.take_along_axis(scores, idx, axis=-1)
    w = w / w.sum(-1, keepdims=True) * ROUTED_SCALE
    return idx, w


def routed_experts(x2, idx, gate, w_gate_e, w_up_e, w_down_e):
    n, d = x2.shape
    a = n * TOP_K
    n_blocks = (a + N_EXPERTS * (EXPERT_BLOCK - 1) + EXPERT_BLOCK - 1) // EXPERT_BLOCK
    rows = n_blocks * EXPERT_BLOCK
    flat_e = idx.reshape(-1)
    flat_g = gate.reshape(-1)
    order = jnp.argsort(flat_e)
    sorted_e = flat_e[order]
    counts = jnp.bincount(flat_e, length=N_EXPERTS)
    starts = jnp.cumsum(counts) - counts
    padded = (counts + EXPERT_BLOCK - 1) // EXPERT_BLOCK * EXPERT_BLOCK
    pad_end = jnp.cumsum(padded)
    pad_start = pad_end - padded
    dest = pad_start[sorted_e] + jnp.arange(a) - starts[sorted_e]
    row_token = jnp.full((rows,), n, jnp.int32).at[dest].set((order // TOP_K).astype(jnp.int32))
    row_gate = jnp.zeros((rows,), jnp.float32).at[dest].set(flat_g[order])
    block_expert = jnp.minimum(
        jnp.searchsorted(pad_end, jnp.arange(n_blocks) * EXPERT_BLOCK, side='right'), N_EXPERTS - 1)
    x_pad = jnp.concatenate([x2, jnp.zeros((1, d), x2.dtype)], axis=0)
    xb = x_pad[row_token].reshape(n_blocks, EXPERT_BLOCK, d)

    def run(args):
        xblk, e = args
        return swiglu(xblk, w_gate_e[e], w_up_e[e], w_down_e[e])

    yb = lax.map(run, (xb, block_expert)).reshape(rows, d)
    return jax.ops.segment_sum(yb * row_gate[:, None], row_token, num_segments=n + 1)[:n]


def moe_ffn(x, w_router, router_bias, w_gate_e, w_up_e, w_down_e, w_gate_s, w_up_s, w_down_s):
    B, T, d = x.shape
    x2 = x.reshape(B * T, d)
    idx, gate = route(x2, w_router, router_bias)
    y = routed_experts(x2, idx, gate, w_gate_e, w_up_e, w_down_e) + swiglu(x2, w_gate_s, w_up_s, w_down_s)
    return y.reshape(B, T, d).astype(x.dtype)


def finish_layer(x, o_a, o_f, w_o, ln1_g, ln1_b, w_router, router_bias,
                 w_gate_e, w_up_e, w_down_e, w_gate_s, w_up_s, w_down_s, ln2_g, ln2_b):
    B, T, _ = x.shape
    heads = jnp.concatenate([o_a.reshape(B, T, W_DSA), o_f.reshape(B, T, W_FOX)], axis=-1).astype(x.dtype)
    mix = jnp.einsum('btc,cd->btd', heads, w_o)
    h = layer_norm(DEEPNORM_ALPHA * x + mix, ln1_g, ln1_b)
    f = moe_ffn(h, w_router, router_bias, w_gate_e, w_up_e, w_down_e, w_gate_s, w_up_s, w_down_s)
    return layer_norm(DEEPNORM_ALPHA * h + f, ln2_g, ln2_b)


def kernel(x_prompt, x_sample, cache_k_dsa, cache_v_dsa, cache_k_idx, cache_k_fox, cache_v_fox,
           cache_logf_fox, page_table, w_in, b_forget, idx_ln_g, idx_ln_b, w_o, ln1_g, ln1_b,
           w_router, router_bias, w_gate_e, w_up_e, w_down_e, w_gate_s, w_up_s, w_down_s,
           ln2_g, ln2_b):
    seq = x_prompt.shape[1]
    dec_seq = x_sample.shape[1]
    past = page_table.shape[1] * PAGE_SIZE
    pos_p = jnp.arange(seq, dtype=jnp.int32)
    pos_s = past + jnp.arange(dec_seq, dtype=jnp.int32)
    xp, xs = x_prompt, x_sample
    l = 0
    ffn = (w_o[l], ln1_g[l], ln1_b[l], w_router[l], router_bias[l], w_gate_e[l], w_up_e[l],
           w_down_e[l], w_gate_s[l], w_up_s[l], w_down_s[l], ln2_g[l], ln2_b[l])
    q_a, k_a, v_a, q_i, k_i, w_i, q_f, k_f, v_f, logf = project_mixers(
        xp, pos_p, w_in[l], b_forget[l], idx_ln_g[l], idx_ln_b[l])
    o_a = dsa_prompt(q_a, k_a, v_a, q_i, k_i, w_i)
    o_f = fox_prompt(q_f, k_f, v_f, logf)
    xp = finish_layer(xp, o_a, o_f, *ffn)
    outs_p = (k_a, v_a, k_i, k_f, v_f, logf.astype(x_prompt.dtype))
    q_a, k_a, v_a, q_i, k_i, w_i, q_f, k_f, v_f, logf = project_mixers(
        xs, pos_s, w_in[l], b_forget[l], idx_ln_g[l], idx_ln_b[l])
    o_a = dsa_sample(q_a, k_a, v_a, q_i, k_i, w_i, cache_k_dsa[l], cache_v_dsa[l], cache_k_idx[l], page_table)
    o_f = fox_sample(q_f, k_f, v_f, logf, cache_k_fox[l], cache_v_fox[l], cache_logf_fox[l], page_table)
    xs = finish_layer(xs, o_a, o_f, *ffn)
    outs_s = (k_a, v_a, k_i, k_f, v_f, logf.astype(x_sample.dtype))
    return (xp, xs) + tuple(o[None] for o in outs_p) + tuple(o[None] for o in outs_s)
```

```python
import functools
import math
import jax
import jax.numpy as jnp
from jax import lax
import numpy as np
from jax.experimental import pallas as pl
from jax.experimental.pallas import tpu as pltpu

D_MODEL = 1024
PAGE_SIZE = 128
HEAD_DIM = 64
N_HEADS = D_MODEL // HEAD_DIM
H_DSA = N_HEADS // 2
H_FOX = N_HEADS - H_DSA
W_DSA = H_DSA * HEAD_DIM
W_FOX = H_FOX * HEAD_DIM
MIX_WIDTH = W_DSA + W_FOX
H_IDX = 8
D_IDX = 64
TOPK_MAX = 256
Q_BLOCK = 128
ROPE_THETA = 10000.0
N_EXPERTS = 256
TOP_K = 8
N_GROUPS = 8
TOPK_GROUPS = 4
D_EXPERT = 256
D_SHARED = 256
ROUTED_SCALE = 2.5
EXPERT_BLOCK = 128
LN_EPS = 1e-5
DEPTH = 1
DEEPNORM_ALPHA = (2 * DEPTH) ** 0.25

_IN_SIZES = (W_DSA, W_DSA, W_DSA, H_IDX * D_IDX, D_IDX, H_IDX, W_FOX, W_FOX, W_FOX, H_FOX)
IN_COLS = sum(_IN_SIZES)
SPLIT_POINTS = tuple(int(v) for v in np.cumsum(_IN_SIZES)[:-1])

LANES = 128


def _proj_body(x_ref, w_ref, o_ref):
    o_ref[...] = jnp.dot(x_ref[...].astype(jnp.bfloat16), w_ref[...],
                         preferred_element_type=jnp.float32)


def _project(x2, w):
    m, d = x2.shape
    c = w.shape[1]
    cp = (c + LANES - 1) // LANES * LANES
    wp = jnp.pad(w, ((0, 0), (0, cp - c))).astype(jnp.bfloat16)
    tm = 256
    out = pl.pallas_call(
        _proj_body,
        grid=(m // tm,),
        in_specs=[pl.BlockSpec((tm, d), lambda i: (i, 0)),
                  pl.BlockSpec((d, cp), lambda i: (0, 0))],
        out_specs=pl.BlockSpec((tm, cp), lambda i: (i, 0)),
        out_shape=jax.ShapeDtypeStruct((m, cp), jnp.float32),
        compiler_params=pltpu.CompilerParams(vmem_limit_bytes=48 * 1024 * 1024),
        name="in_proj",
    )(x2, wp)
    return out[:, :c]


def layer_norm(x, g, b):
    xf = x.astype(jnp.float32)
    mu = xf.mean(-1, keepdims=True)
    var = jnp.square(xf - mu).mean(-1, keepdims=True)
    return ((xf - mu) * lax.rsqrt(var + LN_EPS) * g.astype(jnp.float32) + b.astype(jnp.float32)).astype(x.dtype)


def rope(x, pos):
    half = x.shape[-1] // 2
    freqs = ROPE_THETA ** (-jnp.arange(half, dtype=jnp.float32) / half)
    ang = pos.astype(jnp.float32)[:, None] * freqs[None, :]
    cos = jnp.cos(ang)[None, :, None, :]
    sin = jnp.sin(ang)[None, :, None, :]
    xf = x.astype(jnp.float32)
    x1, x2 = xf[..., :half], xf[..., half:]
    return jnp.concatenate([x1 * cos - x2 * sin, x2 * cos + x1 * sin], axis=-1).astype(x.dtype)


def project_mixers(x, pos, w_in, b_forget, idx_ln_g, idx_ln_b):
    B, T, _ = x.shape
    h = _project(x.reshape(B * T, D_MODEL), w_in).reshape(B, T, IN_COLS)
    q_a, k_a, v_a, q_i, k_i, w_i, q_f, k_f, v_f, f_logit = jnp.split(h, SPLIT_POINTS, axis=-1)
    q_a = rope(q_a.reshape(B, T, H_DSA, HEAD_DIM), pos)
    k_a = rope(k_a.reshape(B, T, H_DSA, HEAD_DIM), pos)
    v_a = v_a.reshape(B, T, H_DSA, HEAD_DIM)
    q_i = rope(q_i.reshape(B, T, H_IDX, D_IDX), pos)
    k_i = rope(layer_norm(k_i, idx_ln_g, idx_ln_b)[:, :, None, :], pos)[:, :, 0, :]
    w_i = w_i.astype(jnp.float32) * (H_IDX ** -0.5 * D_IDX ** -0.5)
    q_f = q_f.reshape(B, T, H_FOX, HEAD_DIM)
    k_f = k_f.reshape(B, T, H_FOX, HEAD_DIM)
    v_f = v_f.reshape(B, T, H_FOX, HEAD_DIM)
    logf = jax.nn.log_sigmoid((f_logit + b_forget).astype(jnp.float32))
    return q_a, k_a, v_a, q_i, k_i, w_i, q_f, k_f, v_f, logf


def indexer_scores(q_i, k_i, w_i):
    s = jnp.einsum('bqhd,bld->bqhl', q_i, k_i).astype(jnp.float32)
    return jnp.einsum('bqhl,bqh->bql', jax.nn.relu(s), w_i)


def select_topk(scores, mask, k):
    masked = jnp.where(mask, scores, -jnp.inf)
    vals, idx = lax.top_k(masked, k)
    return idx, vals > -jnp.inf


def gather_rows(src, idx):
    return jax.vmap(lambda s, i: s[i])(src, idx)


def sparse_attend(q, kg, vg, valid):
    s = jnp.einsum('bqhd,bqkhd->bqhk', q, kg).astype(jnp.float32) * HEAD_DIM ** -0.5
    s = jnp.where(valid[:, :, None, :], s, -jnp.inf)
    p = jax.nn.softmax(s, axis=-1)
    return jnp.einsum('bqhk,bqkhd->bqhd', p, vg)


INT_MIN = -2 ** 31
MASKED_SCORE = -1e30
RUNNING_MAX_INIT = -1e29
NT_DIMS = (((1,), (1,)), ((), ()))


def _head_major(x):
    return x.transpose(0, 2, 1, 3)


def _with_ones(v):
    return jnp.concatenate([v, jnp.ones_like(v)], axis=-1)


def _online_softmax_step(s, vc, m_ref, acc_ref):
    m_old = m_ref[...]
    m_new = jnp.maximum(m_old, jnp.max(s, axis=-1, keepdims=True))
    alpha = jnp.exp(m_old - m_new)
    p = jnp.exp(s - m_new)
    pv = jnp.einsum('hqk,hkd->hqd', p.astype(jnp.bfloat16), vc, preferred_element_type=jnp.float32)
    acc_ref[...] = alpha * acc_ref[...] + pv
    m_ref[...] = m_new


def _init_softmax_state(m_ref, acc_ref):
    m_ref[...] = jnp.full(m_ref.shape, RUNNING_MAX_INIT, jnp.float32)
    acc_ref[...] = jnp.zeros(acc_ref.shape, jnp.float32)


def _write_normalised(acc_ref, o_ref):
    a = acc_ref[...]
    o_ref[0] = a[:, :, :HEAD_DIM] / a[:, :, HEAD_DIM:]


def _dsa_prompt_body(qi_ref, ki_ref, wi_ref, q_ref, k_ref, v_ref, o_ref,
                     keys, wb, m_s, acc_s, *, topk, tq, seq_bits):
    i = pl.program_id(1)
    n_chunks = i + 1
    row = lax.broadcasted_iota(jnp.int32, (tq, tq), 0)
    col = lax.broadcasted_iota(jnp.int32, (tq, tq), 1)
    kf = float(topk)

    w = wi_ref[0]
    for h in range(H_IDX):
        wb[h] = jnp.broadcast_to(w[:, h:h + 1], (tq, tq))
    qi_all = qi_ref[0].reshape(H_IDX * tq, D_IDX)

    def index_chunk(c, carry):
        kic = ki_ref[0, pl.ds(pl.multiple_of(c * tq, tq), tq), :]
        s = lax.dot_general(qi_all, kic, NT_DIMS, preferred_element_type=jnp.float32)
        sc = jnp.maximum(s[0:tq], 0.0) * wb[0]
        for h in range(1, H_IDX):
            sc = sc + jnp.maximum(s[h * tq:(h + 1) * tq], 0.0) * wb[h]
        sc = jnp.where(sc == 0.0, 0.0, sc)
        bits = pltpu.bitcast(sc, jnp.int32)
        keys[c] = jnp.where(bits < 0, bits ^ 0x7FFFFFFF, bits)
        return carry

    lax.fori_loop(0, n_chunks, index_chunk, 0)
    keys[i] = jnp.where(col <= row, keys[i], INT_MIN)

    def count_ge(cand):
        cb = jnp.broadcast_to(cand, (tq, tq))

        def body(c, acc):
            return acc + jnp.where(keys[c] >= cb, 1.0, 0.0)

        acc = lax.fori_loop(0, n_chunks, body, jnp.zeros((tq, tq), jnp.float32))
        return jnp.sum(acc, axis=-1, keepdims=True)

    c0 = count_ge(jnp.zeros((tq, 1), jnp.int32))
    nonneg = c0 >= kf
    prefix = jnp.where(nonneg, 0, INT_MIN).astype(jnp.int32)
    total = (n_chunks * tq).astype(jnp.float32)
    cnt = jnp.where(nonneg, c0, total)

    def bit_step(t, carry):
        prefix, cnt = carry
        cand = prefix + lax.shift_left(jnp.int32(1), 30 - t)
        cc = count_ge(cand)
        take = cc >= kf
        return jnp.where(take, cand, prefix), jnp.where(take, cc, cnt)

    thr, cnt = lax.fori_loop(0, 31, bit_step, (prefix, cnt))
    thr = jnp.maximum(thr, INT_MIN + 1)
    excess = cnt > kf
    any_excess = jnp.max(jnp.where(excess, 1.0, 0.0)) > 0.0

    @pl.when(any_excess)
    def _():
        need = kf - count_ge(thr + 1)
        thb = jnp.broadcast_to(thr, (tq, tq))

        def count_ties_before(j):
            jb = jnp.broadcast_to(j, (tq, tq))

            def body(c, acc):
                hit = (keys[c] == thb) & ((col + c * tq) < jb)
                return acc + jnp.where(hit, 1.0, 0.0)

            acc = lax.fori_loop(0, n_chunks, body, jnp.zeros((tq, tq), jnp.float32))
            return jnp.sum(acc, axis=-1, keepdims=True)

        def idx_step(t, last):
            cand = last + lax.shift_left(jnp.int32(1), seq_bits - 1 - t)
            return jnp.where(count_ties_before(cand) < need, cand, last)

        last = lax.fori_loop(0, seq_bits, idx_step, jnp.zeros((tq, 1), jnp.int32))
        lb = jnp.broadcast_to(last, (tq, tq))
        eb = jnp.broadcast_to(excess, (tq, tq))

        def drop(c, carry):
            kk = keys[c]
            keys[c] = jnp.where(eb & (kk == thb) & ((col + c * tq) > lb), INT_MIN, kk)
            return carry

        lax.fori_loop(0, n_chunks, drop, 0)

    thb = jnp.broadcast_to(thr, (tq, tq))
    _init_softmax_state(m_s, acc_s)
    qh = q_ref[0]

    def attend_chunk(c, carry):
        sel = keys[c] >= thb
        start = pl.multiple_of(c * tq, tq)
        kc = k_ref[0, :, pl.ds(start, tq), :]
        vc = v_ref[0, :, pl.ds(start, tq), :]
        s = jnp.einsum('hqd,hkd->hqk', qh, kc, preferred_element_type=jnp.float32)
        s = jnp.where(sel[None], s, MASKED_SCORE)
        _online_softmax_step(s, vc, m_s, acc_s)
        return carry

    lax.fori_loop(0, n_chunks, attend_chunk, 0)
    _write_normalised(acc_s, o_ref)


def dsa_prompt(q, k, v, q_i, k_i, w_i):
    B, S = q.shape[:2]
    topk = min(TOPK_MAX, S // 4)
    tq = Q_BLOCK
    nq = S // tq
    bf = jnp.bfloat16
    qh = _head_major(q * HEAD_DIM ** -0.5).astype(bf)
    kh = _head_major(k).astype(bf)
    vh = _with_ones(_head_major(v)).astype(bf)
    qih = _head_major(q_i).astype(bf)
    body = functools.partial(_dsa_prompt_body, topk=topk, tq=tq, seq_bits=max(1, (S - 1).bit_length()))
    o = pl.pallas_call(
        body,
        grid=(B, nq),
        in_specs=[
            pl.BlockSpec((1, H_IDX, tq, D_IDX), lambda b, i: (b, 0, i, 0)),
            pl.BlockSpec((1, S, D_IDX), lambda b, i: (b, 0, 0)),
            pl.BlockSpec((1, tq, H_IDX), lambda b, i: (b, i, 0)),
            pl.BlockSpec((1, H_DSA, tq, HEAD_DIM), lambda b, i: (b, 0, i, 0)),
            pl.BlockSpec((1, H_DSA, S, HEAD_DIM), lambda b, i: (b, 0, 0, 0)),
            pl.BlockSpec((1, H_DSA, S, 2 * HEAD_DIM), lambda b, i: (b, 0, 0, 0)),
        ],
        out_specs=pl.BlockSpec((1, H_DSA, tq, HEAD_DIM), lambda b, i: (b, 0, i, 0)),
        out_shape=jax.ShapeDtypeStruct((B, H_DSA, S, HEAD_DIM), jnp.float32),
        scratch_shapes=[
            pltpu.VMEM((nq, tq, tq), jnp.int32),
            pltpu.VMEM((H_IDX, tq, tq), jnp.float32),
            pltpu.VMEM((H_DSA, tq, tq), jnp.float32),
            pltpu.VMEM((H_DSA, tq, 2 * HEAD_DIM), jnp.float32),
        ],
        compiler_params=pltpu.CompilerParams(
            dimension_semantics=("arbitrary", "arbitrary"), vmem_limit_bytes=48 * 1024 * 1024),
        name="dsa_prompt",
    )(qih, k_i.astype(bf), w_i, qh, kh, vh)
    return _head_major(o)


def dsa_sample(q, k_new, v_new, q_i, k_i_new, w_i, cache_k, cache_v, cache_ki, page_table):
    B, T = q.shape[:2]
    n_pages = page_table.shape[1]
    P = n_pages * PAGE_SIZE
    L = P + T
    topk = min(TOPK_MAX, L // 4)
    ki_past = cache_ki[page_table].reshape(B, P, D_IDX)
    ki_all = jnp.concatenate([ki_past, k_i_new.astype(ki_past.dtype)], axis=1)
    scores = indexer_scores(q_i, ki_all, w_i)
    qpos = P + jnp.arange(T)
    kpos = jnp.arange(L)
    idx, valid = select_topk(scores, kpos[None, :] <= qpos[:, None], topk)
    is_past = (idx < P)[..., None, None]
    pidx = jnp.minimum(idx, P - 1)
    phys = jax.vmap(lambda pt, i: pt[i])(page_table, pidx // PAGE_SIZE)
    slot = pidx % PAGE_SIZE
    nidx = jnp.clip(idx - P, 0, T - 1)
    kg = jnp.where(is_past, cache_k[phys, slot], gather_rows(k_new.astype(cache_k.dtype), nidx))
    vg = jnp.where(is_past, cache_v[phys, slot], gather_rows(v_new.astype(cache_v.dtype), nidx))
    return sparse_attend(q, kg, vg, valid)


def _fox_prompt_body(q_ref, k_ref, v_ref, ccol_ref, crow_ref, o_ref, cb, m_s, acc_s, *, tq):
    i = pl.program_id(1)
    row = lax.broadcasted_iota(jnp.int32, (tq, tq), 0)
    col = lax.broadcasted_iota(jnp.int32, (tq, tq), 1)
    ccol = ccol_ref[0]
    for h in range(H_FOX):
        cb[h] = jnp.broadcast_to(ccol[:, h:h + 1], (tq, tq))
    _init_softmax_state(m_s, acc_s)
    qh = q_ref[0]

    def attend_chunk(c, diagonal):
        start = pl.multiple_of(c * tq, tq)
        crow = crow_ref[0, :, pl.ds(start, tq)]
        kc = k_ref[0, :, pl.ds(start, tq), :]
        vc = v_ref[0, :, pl.ds(start, tq), :]
        s = jnp.einsum('hqd,hkd->hqk', qh, kc, preferred_element_type=jnp.float32)
        s = s + cb[...] - crow[:, None, :]
        if diagonal:
            s = jnp.where((col <= row)[None], s, MASKED_SCORE)
        _online_softmax_step(s, vc, m_s, acc_s)

    def full_chunk(c, carry):
        attend_chunk(c, False)
        return carry

    lax.fori_loop(0, i, full_chunk, 0)
    attend_chunk(i, True)
    _write_normalised(acc_s, o_ref)


def fox_prompt(q, k, v, logf):
    B, S = q.shape[:2]
    tq = Q_BLOCK
    bf = jnp.bfloat16
    c = jnp.cumsum(logf, axis=1)
    qh = _head_major(q * HEAD_DIM ** -0.5).astype(bf)
    kh = _head_major(k).astype(bf)
    vh = _with_ones(_head_major(v)).astype(bf)
    o = pl.pallas_call(
        functools.partial(_fox_prompt_body, tq=tq),
        grid=(B, S // tq),
        in_specs=[
            pl.BlockSpec((1, H_FOX, tq, HEAD_DIM), lambda b, i: (b, 0, i, 0)),
            pl.BlockSpec((1, H_FOX, S, HEAD_DIM), lambda b, i: (b, 0, 0, 0)),
            pl.BlockSpec((1, H_FOX, S, 2 * HEAD_DIM), lambda b, i: (b, 0, 0, 0)),
            pl.BlockSpec((1, tq, H_FOX), lambda b, i: (b, i, 0)),
            pl.BlockSpec((1, H_FOX, S), lambda b, i: (b, 0, 0)),
        ],
        out_specs=pl.BlockSpec((1, H_FOX, tq, HEAD_DIM), lambda b, i: (b, 0, i, 0)),
        out_shape=jax.ShapeDtypeStruct((B, H_FOX, S, HEAD_DIM), jnp.float32),
        scratch_shapes=[
            pltpu.VMEM((H_FOX, tq, tq), jnp.float32),
            pltpu.VMEM((H_FOX, tq, tq), jnp.float32),
            pltpu.VMEM((H_FOX, tq, 2 * HEAD_DIM), jnp.float32),
        ],
        compiler_params=pltpu.CompilerParams(
            dimension_semantics=("arbitrary", "arbitrary"), vmem_limit_bytes=48 * 1024 * 1024),
        name="fox_prompt",
    )(qh, kh, vh, c, c.transpose(0, 2, 1))
    return _head_major(o)


def fox_sample(q, k_new, v_new, logf_new, cache_k, cache_v, cache_logf, page_table):
    B, T = q.shape[:2]
    n_pages = page_table.shape[1]
    P = n_pages * PAGE_SIZE
    scale = HEAD_DIM ** -0.5
    logf_past = cache_logf[page_table].reshape(B, P, H_FOX).astype(jnp.float32)
    r = (jnp.cumsum(logf_past[:, ::-1], axis=1)[:, ::-1] - logf_past).transpose(0, 2, 1)
    cn = jnp.cumsum(logf_new, axis=1).transpose(0, 2, 1)
    s_new = (jnp.einsum('bthd,bshd->bhts', q, k_new).astype(jnp.float32) * scale
             + cn[:, :, :, None] - cn[:, :, None, :])
    tpos = jnp.arange(T)
    s_new = jnp.where(tpos[None, :] <= tpos[:, None], s_new, -jnp.inf)
    m_new = s_new.max(-1)
    e_new = jnp.exp(s_new - m_new[..., None])
    l_new = e_new.sum(-1)
    acc_new = jnp.einsum('bhts,bshd->bhtd', e_new, v_new)

    def page(p):
        phys = page_table[:, p]
        kp = cache_k[phys]
        vp = cache_v[phys]
        rp = lax.dynamic_slice_in_dim(r, p * PAGE_SIZE, PAGE_SIZE, axis=2)
        s = (jnp.einsum('bthd,bshd->bhts', q, kp).astype(jnp.float32) * scale
             + cn[:, :, :, None] + rp[:, :, None, :])
        m = s.max(-1)
        e = jnp.exp(s - m[..., None])
        return m, e.sum(-1), jnp.einsum('bhts,bshd->bhtd', e, vp)

    m_p, l_p, acc_p = lax.map(page, jnp.arange(n_pages))
    m = jnp.maximum(m_p.max(0), m_new)
    sc_p = jnp.exp(m_p - m[None])
    sc_n = jnp.exp(m_new - m)
    l = (l_p * sc_p).sum(0) + l_new * sc_n
    acc = (acc_p * sc_p[..., None]).sum(0) + acc_new * sc_n[..., None]
    return (acc / l[..., None]).transpose(0, 2, 1, 3)


def swiglu(x, wg, wu, wd):
    return (jax.nn.silu(x @ wg) * (x @ wu)) @ wd


def route(x2, w_router, router_bias):
    n = x2.shape[0]
    scores = jax.nn.sigmoid(jnp.einsum('nd,de->ne', x2, w_router).astype(jnp.float32))
    biased = scores + router_bias.astype(jnp.float32)
    grp = biased.reshape(n, N_GROUPS, N_EXPERTS // N_GROUPS)
    grp_score = lax.top_k(grp, 2)[0].sum(-1)
    _, top_g = lax.top_k(grp_score, TOPK_GROUPS)
    keep_g = (top_g[:, :, None] == jnp.arange(N_GROUPS)[None, None, :]).any(1)
    masked = jnp.where(keep_g[:, :, None], grp, -jnp.inf).reshape(n, N_EXPERTS)
    _, idx = lax.top_k(masked, TOP_K)
    w = jnp.take_along_axis(scores, idx, axis=-1)
    w = w / w.sum(-1, keepdims=True) * ROUTED_SCALE
    return idx, w


def routed_experts(x2, idx, gate, w_gate_e, w_up_e, w_down_e):
    n, d = x2.shape
    a = n * TOP_K
    n_blocks = (a + N_EXPERTS * (EXPERT_BLOCK - 1) + EXPERT_BLOCK - 1) // EXPERT_BLOCK
    rows = n_blocks * EXPERT_BLOCK
    flat_e = idx.reshape(-1)
    flat_g = gate.reshape(-1)
    order = jnp.argsort(flat_e)
    sorted_e = flat_e[order]
    counts = jnp.bincount(flat_e, length=N_EXPERTS)
    starts = jnp.cumsum(counts) - counts
    padded = (counts + EXPERT_BLOCK - 1) // EXPERT_BLOCK * EXPERT_BLOCK
    pad_end = jnp.cumsum(padded)
    pad_start = pad_end - padded
    dest = pad_start[sorted_e] + jnp.arange(a) - starts[sorted_e]
    row_token = jnp.full((rows,), n, jnp.int32).at[dest].set((order // TOP_K).astype(jnp.int32))
    row_gate = jnp.zeros((rows,), jnp.float32).at[dest].set(flat_g[order])
    block_expert = jnp.minimum(
        jnp.searchsorted(pad_end, jnp.arange(n_blocks) * EXPERT_BLOCK, side='right'), N_EXPERTS - 1)
    x_pad = jnp.concatenate([x2, jnp.zeros((1, d), x2.dtype)], axis=0)
    xb = x_pad[row_token].reshape(n_blocks, EXPERT_BLOCK, d)

    def run(args):
        xblk, e = args
        return swiglu(xblk, w_gate_e[e], w_up_e[e], w_down_e[e])

    yb = lax.map(run, (xb, block_expert)).reshape(rows, d)
    return jax.ops.segment_sum(yb * row_gate[:, None], row_token, num_segments=n + 1)[:n]


def moe_ffn(x, w_router, router_bias, w_gate_e, w_up_e, w_down_e, w_gate_s, w_up_s, w_down_s):
    B, T, d = x.shape
    x2 = x.reshape(B * T, d)
    idx, gate = route(x2, w_router, router_bias)
    y = routed_experts(x2, idx, gate, w_gate_e, w_up_e, w_down_e) + swiglu(x2, w_gate_s, w_up_s, w_down_s)
    return y.reshape(B, T, d).astype(x.dtype)


def finish_layer(x, o_a, o_f, w_o, ln1_g, ln1_b, w_router, router_bias,
                 w_gate_e, w_up_e, w_down_e, w_gate_s, w_up_s, w_down_s, ln2_g, ln2_b):
    B, T, _ = x.shape
    heads = jnp.concatenate([o_a.reshape(B, T, W_DSA), o_f.reshape(B, T, W_FOX)], axis=-1).astype(x.dtype)
    mix = jnp.einsum('btc,cd->btd', heads, w_o)
    h = layer_norm(DEEPNORM_ALPHA * x + mix, ln1_g, ln1_b)
    f = moe_ffn(h, w_router, router_bias, w_gate_e, w_up_e, w_down_e, w_gate_s, w_up_s, w_down_s)
    return layer_norm(DEEPNORM_ALPHA * h + f, ln2_g, ln2_b)


def kernel(x_prompt, x_sample, cache_k_dsa, cache_v_dsa, cache_k_idx, cache_k_fox, cache_v_fox,
           cache_logf_fox, page_table, w_in, b_forget, idx_ln_g, idx_ln_b, w_o, ln1_g, ln1_b,
           w_router, router_bias, w_gate_e, w_up_e, w_down_e, w_gate_s, w_up_s, w_down_s,
           ln2_g, ln2_b):
    seq = x_prompt.shape[1]
    dec_seq = x_sample.shape[1]
    past = page_table.shape[1] * PAGE_SIZE
    pos_p = jnp.arange(seq, dtype=jnp.int32)
    pos_s = past + jnp.arange(dec_seq, dtype=jnp.int32)
    xp, xs = x_prompt, x_sample
    l = 0
    ffn = (w_o[l], ln1_g[l], ln1_b[l], w_router[l], router_bias[l], w_gate_e[l], w_up_e[l],
           w_down_e[l], w_gate_s[l], w_up_s[l], w_down_s[l], ln2_g[l], ln2_b[l])
    q_a, k_a, v_a, q_i, k_i, w_i, q_f, k_f, v_f, logf = project_mixers(
        xp, pos_p, w_in[l], b_forget[l], idx_ln_g[l], idx_ln_b[l])
    o_a = dsa_prompt(q_a, k_a, v_a, q_i, k_i, w_i)
    o_f = fox_prompt(q_f, k_f, v_f, logf)
    xp = finish_layer(xp, o_a, o_f, *ffn)
    outs_p = (k_a, v_a, k_i, k_f, v_f, logf.astype(x_prompt.dtype))
    q_a, k_a, v_a, q_i, k_i, w_i, q_f, k_f, v_f, logf = project_mixers(
        xs, pos_s, w_in[l], b_forget[l], idx_ln_g[l], idx_ln_b[l])
    o_a = dsa_sample(q_a, k_a, v_a, q_i, k_i, w_i, cache_k_dsa[l], cache_v_dsa[l], cache_k_idx[l], page_table)
    o_f = fox_sample(q_f, k_f, v_f, logf, cache_k_fox[l], cache_v_fox[l], cache_logf_fox[l], page_table)
    xs = finish_layer(xs, o_a, o_f, *ffn)
    outs_s = (k_a, v_a, k_i, k_f, v_f, logf.astype(x_sample.dtype))
    return (xp, xs) + tuple(o[None] for o in outs_p) + tuple(o[None] for o in outs_s)
```

```python
import functools
import math
import jax
import jax.numpy as jnp
from jax import lax
import numpy as np
from jax.experimental import pallas as pl
from jax.experimental.pallas import tpu as pltpu

D_MODEL = 1024
PAGE_SIZE = 128
HEAD_DIM = 64
N_HEADS = D_MODEL // HEAD_DIM
H_DSA = N_HEADS // 2
H_FOX = N_HEADS - H_DSA
W_DSA = H_DSA * HEAD_DIM
W_FOX = H_FOX * HEAD_DIM
MIX_WIDTH = W_DSA + W_FOX
H_IDX = 8
D_IDX = 64
TOPK_MAX = 256
Q_BLOCK = 128
ROPE_THETA = 10000.0
N_EXPERTS = 256
TOP_K = 8
N_GROUPS = 8
TOPK_GROUPS = 4
D_EXPERT = 256
D_SHARED = 256
ROUTED_SCALE = 2.5
EXPERT_BLOCK = 128
LN_EPS = 1e-5
DEPTH = 1
DEEPNORM_ALPHA = (2 * DEPTH) ** 0.25

_IN_SIZES = (W_DSA, W_DSA, W_DSA, H_IDX * D_IDX, D_IDX, H_IDX, W_FOX, W_FOX, W_FOX, H_FOX)
IN_COLS = sum(_IN_SIZES)
SPLIT_POINTS = tuple(int(v) for v in np.cumsum(_IN_SIZES)[:-1])

LANES = 128


def _proj_body(x_ref, w_ref, o_ref):
    o_ref[...] = jnp.dot(x_ref[...].astype(jnp.bfloat16), w_ref[...],
                         preferred_element_type=jnp.float32)


def _project(x2, w):
    m, d = x2.shape
    c = w.shape[1]
    cp = (c + LANES - 1) // LANES * LANES
    wp = jnp.pad(w, ((0, 0), (0, cp - c))).astype(jnp.bfloat16)
    tm = 256
    out = pl.pallas_call(
        _proj_body,
        grid=(m // tm,),
        in_specs=[pl.BlockSpec((tm, d), lambda i: (i, 0)),
                  pl.BlockSpec((d, cp), lambda i: (0, 0))],
        out_specs=pl.BlockSpec((tm, cp), lambda i: (i, 0)),
        out_shape=jax.ShapeDtypeStruct((m, cp), jnp.float32),
        compiler_params=pltpu.CompilerParams(vmem_limit_bytes=48 * 1024 * 1024),
        name="in_proj",
    )(x2, wp)
    return out[:, :c]


def layer_norm(x, g, b):
    xf = x.astype(jnp.float32)
    mu = xf.mean(-1, keepdims=True)
    var = jnp.square(xf - mu).mean(-1, keepdims=True)
    return ((xf - mu) * lax.rsqrt(var + LN_EPS) * g.astype(jnp.float32) + b.astype(jnp.float32)).astype(x.dtype)


def rope(x, pos):
    half = x.shape[-1] // 2
    freqs = ROPE_THETA ** (-jnp.arange(half, dtype=jnp.float32) / half)
    ang = pos.astype(jnp.float32)[:, None] * freqs[None, :]
    cos = jnp.cos(ang)[None, :, None, :]
    sin = jnp.sin(ang)[None, :, None, :]
    xf = x.astype(jnp.float32)
    x1, x2 = xf[..., :half], xf[..., half:]
    return jnp.concatenate([x1 * cos - x2 * sin, x2 * cos + x1 * sin], axis=-1).astype(x.dtype)


def project_mixers(x, pos, w_in, b_forget, idx_ln_g, idx_ln_b):
    B, T, _ = x.shape
    h = _project(x.reshape(B * T, D_MODEL), w_in).reshape(B, T, IN_COLS)
    q_a, k_a, v_a, q_i, k_i, w_i, q_f, k_f, v_f, f_logit = jnp.split(h, SPLIT_POINTS, axis=-1)
    q_a = rope(q_a.reshape(B, T, H_DSA, HEAD_DIM), pos)
    k_a = rope(k_a.reshape(B, T, H_DSA, HEAD_DIM), pos)
    v_a = v_a.reshape(B, T, H_DSA, HEAD_DIM)
    q_i = rope(q_i.reshape(B, T, H_IDX, D_IDX), pos)
    k_i = rope(layer_norm(k_i, idx_ln_g, idx_ln_b)[:, :, None, :], pos)[:, :, 0, :]
    w_i = w_i.astype(jnp.float32) * (H_IDX ** -0.5 * D_IDX ** -0.5)
    q_f = q_f.reshape(B, T, H_FOX, HEAD_DIM)
    k_f = k_f.reshape(B, T, H_FOX, HEAD_DIM)
    v_f = v_f.reshape(B, T, H_FOX, HEAD_DIM)
    logf = jax.nn.log_sigmoid((f_logit + b_forget).astype(jnp.float32))
    return q_a, k_a, v_a, q_i, k_i, w_i, q_f, k_f, v_f, logf


def indexer_scores(q_i, k_i, w_i):
    s = jnp.einsum('bqhd,bld->bqhl', q_i, k_i).astype(jnp.float32)
    return jnp.einsum('bqhl,bqh->bql', jax.nn.relu(s), w_i)


def select_topk(scores, mask, k):
    masked = jnp.where(mask, scores, -jnp.inf)
    vals, idx = lax.top_k(masked, k)
    return idx, vals > -jnp.inf


def gather_rows(src, idx):
    return jax.vmap(lambda s, i: s[i])(src, idx)


def sparse_attend(q, kg, vg, valid):
    s = jnp.einsum('bqhd,bqkhd->bqhk', q, kg).astype(jnp.float32) * HEAD_DIM ** -0.5
    s = jnp.where(valid[:, :, None, :], s, -jnp.inf)
    p = jax.nn.softmax(s, axis=-1)
    return jnp.einsum('bqhk,bqkhd->bqhd', p, vg)


INT_MIN = -2 ** 31
MASKED_SCORE = -1e30
RUNNING_MAX_INIT = -1e29
NT_DIMS = (((1,), (1,)), ((), ()))


def _head_major(x):
    return x.transpose(0, 2, 1, 3)


def _with_ones(v):
    return jnp.concatenate([v, jnp.ones_like(v)], axis=-1)


def _online_softmax_step(s, vc, m_ref, acc_ref):
    m_old = m_ref[...]
    m_new = jnp.maximum(m_old, jnp.max(s, axis=-1, keepdims=True))
    alpha = jnp.exp(m_old - m_new)
    p = jnp.exp(s - m_new)
    pv = jnp.einsum('hqk,hkd->hqd', p.astype(jnp.bfloat16), vc, preferred_element_type=jnp.float32)
    acc_ref[...] = alpha * acc_ref[...] + pv
    m_ref[...] = m_new


def _init_softmax_state(m_ref, acc_ref):
    m_ref[...] = jnp.full(m_ref.shape, RUNNING_MAX_INIT, jnp.float32)
    acc_ref[...] = jnp.zeros(acc_ref.shape, jnp.float32)


def _write_normalised(acc_ref, o_ref):
    a = acc_ref[...]
    o_ref[0] = a[:, :, :HEAD_DIM] / a[:, :, HEAD_DIM:]


def _kth_largest_key(count_ge, rows, kf, total):
    c0 = count_ge(jnp.zeros((rows, 1), jnp.int32))
    nonneg = c0 >= kf
    prefix = jnp.where(nonneg, 0, INT_MIN).astype(jnp.int32)
    cnt = jnp.where(nonneg, c0, total)

    def bit_step(t, carry):
        prefix, cnt = carry
        cand = prefix + lax.shift_left(jnp.int32(1), 30 - t)
        cc = count_ge(cand)
        take = cc >= kf
        return jnp.where(take, cand, prefix), jnp.where(take, cc, cnt)

    thr, cnt = lax.fori_loop(0, 31, bit_step, (prefix, cnt))
    thr = jnp.maximum(thr, INT_MIN + 1)
    excess = cnt > kf
    any_excess = jnp.max(jnp.where(excess, 1.0, 0.0)) > 0.0
    return thr, excess, any_excess


def _last_kept_tie(count_ties_before, need, rows, index_bits):
    def idx_step(t, last):
        cand = last + lax.shift_left(jnp.int32(1), index_bits - 1 - t)
        return jnp.where(count_ties_before(cand) < need, cand, last)

    return lax.fori_loop(0, index_bits, idx_step, jnp.zeros((rows, 1), jnp.int32))


def _dsa_prompt_body(qi_ref, ki_ref, wi_ref, q_ref, k_ref, v_ref, o_ref,
                     keys, wb, m_s, acc_s, *, topk, tq, seq_bits):
    i = pl.program_id(1)
    n_chunks = i + 1
    row = lax.broadcasted_iota(jnp.int32, (tq, tq), 0)
    col = lax.broadcasted_iota(jnp.int32, (tq, tq), 1)
    kf = float(topk)

    w = wi_ref[0]
    for h in range(H_IDX):
        wb[h] = jnp.broadcast_to(w[:, h:h + 1], (tq, tq))
    qi_all = qi_ref[0].reshape(H_IDX * tq, D_IDX)

    def index_chunk(c, carry):
        kic = ki_ref[0, pl.ds(pl.multiple_of(c * tq, tq), tq), :]
        s = lax.dot_general(qi_all, kic, NT_DIMS, preferred_element_type=jnp.float32)
        sc = jnp.maximum(s[0:tq], 0.0) * wb[0]
        for h in range(1, H_IDX):
            sc = sc + jnp.maximum(s[h * tq:(h + 1) * tq], 0.0) * wb[h]
        sc = jnp.where(sc == 0.0, 0.0, sc)
        bits = pltpu.bitcast(sc, jnp.int32)
        keys[c] = jnp.where(bits < 0, bits ^ 0x7FFFFFFF, bits)
        return carry

    lax.fori_loop(0, n_chunks, index_chunk, 0)
    keys[i] = jnp.where(col <= row, keys[i], INT_MIN)

    def count_ge(cand):
        cb = jnp.broadcast_to(cand, (tq, tq))

        def body(c, acc):
            return acc + jnp.where(keys[c] >= cb, 1.0, 0.0)

        acc = lax.fori_loop(0, n_chunks, body, jnp.zeros((tq, tq), jnp.float32))
        return jnp.sum(acc, axis=-1, keepdims=True)

    thr, excess, any_excess = _kth_largest_key(count_ge, tq, kf, (n_chunks * tq).astype(jnp.float32))

    @pl.when(any_excess)
    def _():
        need = kf - count_ge(thr + 1)
        thb = jnp.broadcast_to(thr, (tq, tq))

        def count_ties_before(j):
            jb = jnp.broadcast_to(j, (tq, tq))

            def body(c, acc):
                hit = (keys[c] == thb) & ((col + c * tq) < jb)
                return acc + jnp.where(hit, 1.0, 0.0)

            acc = lax.fori_loop(0, n_chunks, body, jnp.zeros((tq, tq), jnp.float32))
            return jnp.sum(acc, axis=-1, keepdims=True)

        last = _last_kept_tie(count_ties_before, need, tq, seq_bits)
        lb = jnp.broadcast_to(last, (tq, tq))
        eb = jnp.broadcast_to(excess, (tq, tq))

        def drop(c, carry):
            kk = keys[c]
            keys[c] = jnp.where(eb & (kk == thb) & ((col + c * tq) > lb), INT_MIN, kk)
            return carry

        lax.fori_loop(0, n_chunks, drop, 0)

    thb = jnp.broadcast_to(thr, (tq, tq))
    _init_softmax_state(m_s, acc_s)
    qh = q_ref[0]

    def attend_chunk(c, carry):
        sel = keys[c] >= thb
        start = pl.multiple_of(c * tq, tq)
        kc = k_ref[0, :, pl.ds(start, tq), :]
        vc = v_ref[0, :, pl.ds(start, tq), :]
        s = jnp.einsum('hqd,hkd->hqk', qh, kc, preferred_element_type=jnp.float32)
        s = jnp.where(sel[None], s, MASKED_SCORE)
        _online_softmax_step(s, vc, m_s, acc_s)
        return carry

    lax.fori_loop(0, n_chunks, attend_chunk, 0)
    _write_normalised(acc_s, o_ref)


def dsa_prompt(q, k, v, q_i, k_i, w_i):
    B, S = q.shape[:2]
    topk = min(TOPK_MAX, S // 4)
    tq = Q_BLOCK
    nq = S // tq
    bf = jnp.bfloat16
    qh = _head_major(q * HEAD_DIM ** -0.5).astype(bf)
    kh = _head_major(k).astype(bf)
    vh = _with_ones(_head_major(v)).astype(bf)
    qih = _head_major(q_i).astype(bf)
    body = functools.partial(_dsa_prompt_body, topk=topk, tq=tq, seq_bits=max(1, (S - 1).bit_length()))
    o = pl.pallas_call(
        body,
        grid=(B, nq),
        in_specs=[
            pl.BlockSpec((1, H_IDX, tq, D_IDX), lambda b, i: (b, 0, i, 0)),
            pl.BlockSpec((1, S, D_IDX), lambda b, i: (b, 0, 0)),
            pl.BlockSpec((1, tq, H_IDX), lambda b, i: (b, i, 0)),
            pl.BlockSpec((1, H_DSA, tq, HEAD_DIM), lambda b, i: (b, 0, i, 0)),
            pl.BlockSpec((1, H_DSA, S, HEAD_DIM), lambda b, i: (b, 0, 0, 0)),
            pl.BlockSpec((1, H_DSA, S, 2 * HEAD_DIM), lambda b, i: (b, 0, 0, 0)),
        ],
        out_specs=pl.BlockSpec((1, H_DSA, tq, HEAD_DIM), lambda b, i: (b, 0, i, 0)),
        out_shape=jax.ShapeDtypeStruct((B, H_DSA, S, HEAD_DIM), jnp.float32),
        scratch_shapes=[
            pltpu.VMEM((nq, tq, tq), jnp.int32),
            pltpu.VMEM((H_IDX, tq, tq), jnp.float32),
            pltpu.VMEM((H_DSA, tq, tq), jnp.float32),
            pltpu.VMEM((H_DSA, tq, 2 * HEAD_DIM), jnp.float32),
        ],
        compiler_params=pltpu.CompilerParams(
            dimension_semantics=("arbitrary", "arbitrary"), vmem_limit_bytes=48 * 1024 * 1024),
        name="dsa_prompt",
    )(qih, k_i.astype(bf), w_i, qh, kh, vh)
    return _head_major(o)


def _score_keys(s, wcol, nt):
    sc = jnp.maximum(s[0:nt], 0.0) * wcol[0:nt]
    for h in range(1, H_IDX):
        sc = sc + jnp.maximum(s[h * nt:(h + 1) * nt], 0.0) * wcol[h * nt:(h + 1) * nt]
    sc = jnp.where(sc == 0.0, 0.0, sc)
    bits = pltpu.bitcast(sc, jnp.int32)
    return jnp.where(bits < 0, bits ^ 0x7FFFFFFF, bits)


def _idx_sample_body(pt_ref, qi_ref, wc_ref, kin_ref, ex_ref, *rest, g, n_pages, nt, topk, index_bits):
    ki_refs = rest[:g]
    pb_ref, sn_ref, keys = rest[g:]
    j = pl.program_id(1)
    bf = jnp.bfloat16
    slots = keys.shape[2]
    qi = qi_ref[0]
    wcol = wc_ref[0]
    for gi in range(g):
        s = lax.dot_general(qi, ki_refs[gi][0].astype(bf), NT_DIMS, preferred_element_type=jnp.float32)
        keys[j * g + gi] = _score_keys(s, wcol, nt)

    @pl.when(j == pl.num_programs(1) - 1)
    def _():
        row = lax.broadcasted_iota(jnp.int32, (nt, slots), 0)
        col = lax.broadcasted_iota(jnp.int32, (nt, slots), 1)
        s = lax.dot_general(qi, kin_ref[0], NT_DIMS, preferred_element_type=jnp.float32)
        keys[n_pages] = jnp.where(col <= row, _score_keys(s, wcol, nt), INT_MIN)
        kf = float(topk)

        def count_ge(cand):
            hit = jnp.where(keys[...] >= cand[None], 1.0, 0.0)
            return jnp.sum(jnp.sum(hit, axis=0), axis=-1, keepdims=True)

        total = jnp.float32((n_pages + 1) * slots)
        thr, excess, any_excess = _kth_largest_key(count_ge, nt, kf, total)

        @pl.when(any_excess)
        def _():
            need = kf - count_ge(thr + 1)
            pos = (lax.broadcasted_iota(jnp.int32, keys.shape, 0) * slots
                   + lax.broadcasted_iota(jnp.int32, keys.shape, 2))

            def count_ties_before(jj):
                hit = jnp.where((keys[...] == thr[None]) & (pos < jj[None]), 1.0, 0.0)
                return jnp.sum(jnp.sum(hit, axis=0), axis=-1, keepdims=True)

            last = _last_kept_tie(count_ties_before, need, nt, index_bits)
            kk = keys[...]
            keys[...] = jnp.where(excess[None] & (kk == thr[None]) & (pos > last[None]), INT_MIN, kk)

        sel = jnp.where(keys[...] >= thr[None], 1.0, 0.0)
        sn_ref[0] = sel[n_pages]
        past = sel[:n_pages].reshape(n_pages * nt, slots).astype(bf)
        rep = jnp.dot(past, ex_ref[...], preferred_element_type=jnp.float32)
        pb_ref[0] = ((rep - 1.0) * (-MASKED_SCORE)).reshape(n_pages, nt, ex_ref.shape[1])


def dsa_sample(q, k_new, v_new, q_i, k_i_new, w_i, cache_k, cache_v, cache_ki, page_table):
    B, T, H, D = q.shape
    n_pages = page_table.shape[1]
    slots = cache_ki.shape[1]
    L = n_pages * slots + T
    topk = min(TOPK_MAX, L // 4)
    g = IDX_PAGES_PER_STEP
    bf = jnp.bfloat16
    qi = _head_major(q_i).reshape(B, H_IDX * T, D_IDX).astype(bf)
    wcol = w_i.transpose(0, 2, 1).reshape(B, H_IDX * T, 1)
    ki_new = jnp.pad(k_i_new, ((0, 0), (0, slots - T), (0, 0))).astype(bf)
    expand = (jnp.arange(slots)[:, None] == (jnp.arange(slots * H) // H)[None, :]).astype(bf)

    def page_map(gi):
        return lambda b, j, pt: (pt[b, j * g + gi], 0, 0)

    per_batch = lambda b, j, pt: (b, 0, 0)
    grid_spec = pltpu.PrefetchScalarGridSpec(
        num_scalar_prefetch=1,
        grid=(B, n_pages // g),
        in_specs=[
            pl.BlockSpec((1, H_IDX * T, D_IDX), per_batch),
            pl.BlockSpec((1, H_IDX * T, 1), per_batch),
            pl.BlockSpec((1, slots, D_IDX), per_batch),
            pl.BlockSpec((slots, slots * H), lambda b, j, pt: (0, 0)),
        ] + [pl.BlockSpec((1, slots, D_IDX), page_map(gi)) for gi in range(g)],
        out_specs=[
            pl.BlockSpec((1, n_pages, T, slots * H), lambda b, j, pt: (b, 0, 0, 0)),
            pl.BlockSpec((1, T, slots), per_batch),
        ],
        scratch_shapes=[pltpu.VMEM((n_pages + 1, T, slots), jnp.int32)],
    )
    page_bias, sel_new = pl.pallas_call(
        functools.partial(_idx_sample_body, g=g, n_pages=n_pages, nt=T, topk=topk,
                          index_bits=max(1, ((n_pages + 1) * slots - 1).bit_length())),
        grid_spec=grid_spec,
        out_shape=[jax.ShapeDtypeStruct((B, n_pages, T, slots * H), jnp.float32),
                   jax.ShapeDtypeStruct((B, T, slots), jnp.float32)],
        compiler_params=pltpu.CompilerParams(
            dimension_semantics=("arbitrary", "arbitrary"), vmem_limit_bytes=48 * 1024 * 1024),
        name="idx_sample",
    )(page_table, qi, wcol, ki_new, expand, *([cache_ki] * g))
    sel = sel_new[:, :, :T] > 0.0
    same = jnp.arange(H)[:, None] == jnp.arange(H)[None, :]
    nb = jnp.where(same[None, :, None, None, :] & sel[:, None, :, :, None], 0.0, MASKED_SCORE)
    new_bias = nb.reshape(B, H * T, T * H).astype(jnp.float32)
    row_bias = jnp.zeros((B, H * T, 1), jnp.float32)
    return paged_attention(q, k_new, v_new, cache_k, cache_v, page_table, row_bias, page_bias, new_bias)


def _fox_prompt_body(q_ref, k_ref, v_ref, ccol_ref, crow_ref, o_ref, cb, m_s, acc_s, *, tq):
    i = pl.program_id(1)
    row = lax.broadcasted_iota(jnp.int32, (tq, tq), 0)
    col = lax.broadcasted_iota(jnp.int32, (tq, tq), 1)
    ccol = ccol_ref[0]
    for h in range(H_FOX):
        cb[h] = jnp.broadcast_to(ccol[:, h:h + 1], (tq, tq))
    _init_softmax_state(m_s, acc_s)
    qh = q_ref[0]

    def attend_chunk(c, diagonal):
        start = pl.multiple_of(c * tq, tq)
        crow = crow_ref[0, :, pl.ds(start, tq)]
        kc = k_ref[0, :, pl.ds(start, tq), :]
        vc = v_ref[0, :, pl.ds(start, tq), :]
        s = jnp.einsum('hqd,hkd->hqk', qh, kc, preferred_element_type=jnp.float32)
        s = s + cb[...] - crow[:, None, :]
        if diagonal:
            s = jnp.where((col <= row)[None], s, MASKED_SCORE)
        _online_softmax_step(s, vc, m_s, acc_s)

    def full_chunk(c, carry):
        attend_chunk(c, False)
        return carry

    lax.fori_loop(0, i, full_chunk, 0)
    attend_chunk(i, True)
    _write_normalised(acc_s, o_ref)


def fox_prompt(q, k, v, logf):
    B, S = q.shape[:2]
    tq = Q_BLOCK
    bf = jnp.bfloat16
    c = jnp.cumsum(logf, axis=1)
    qh = _head_major(q * HEAD_DIM ** -0.5).astype(bf)
    kh = _head_major(k).astype(bf)
    vh = _with_ones(_head_major(v)).astype(bf)
    o = pl.pallas_call(
        functools.partial(_fox_prompt_body, tq=tq),
        grid=(B, S // tq),
        in_specs=[
            pl.BlockSpec((1, H_FOX, tq, HEAD_DIM), lambda b, i: (b, 0, i, 0)),
            pl.BlockSpec((1, H_FOX, S, HEAD_DIM), lambda b, i: (b, 0, 0, 0)),
            pl.BlockSpec((1, H_FOX, S, 2 * HEAD_DIM), lambda b, i: (b, 0, 0, 0)),
            pl.BlockSpec((1, tq, H_FOX), lambda b, i: (b, i, 0)),
            pl.BlockSpec((1, H_FOX, S), lambda b, i: (b, 0, 0)),
        ],
        out_specs=pl.BlockSpec((1, H_FOX, tq, HEAD_DIM), lambda b, i: (b, 0, i, 0)),
        out_shape=jax.ShapeDtypeStruct((B, H_FOX, S, HEAD_DIM), jnp.float32),
        scratch_shapes=[
            pltpu.VMEM((H_FOX, tq, tq), jnp.float32),
            pltpu.VMEM((H_FOX, tq, tq), jnp.float32),
            pltpu.VMEM((H_FOX, tq, 2 * HEAD_DIM), jnp.float32),
        ],
        compiler_params=pltpu.CompilerParams(
            dimension_semantics=("arbitrary", "arbitrary"), vmem_limit_bytes=48 * 1024 * 1024),
        name="fox_prompt",
    )(qh, kh, vh, c, c.transpose(0, 2, 1))
    return _head_major(o)


PAGES_PER_STEP = 8
IDX_PAGES_PER_STEP = 16


def _softmax_rows_update(s, v2, m_s, l_s, acc_s):
    m_old = m_s[...]
    m_new = jnp.maximum(m_old, jnp.max(s, axis=-1, keepdims=True))
    alpha = jnp.exp(m_old - m_new)
    p = jnp.exp(s - m_new)
    l_s[...] = alpha * l_s[...] + jnp.sum(p, axis=-1, keepdims=True)
    acc_s[...] = alpha * acc_s[...] + jnp.dot(p.astype(jnp.bfloat16), v2, preferred_element_type=jnp.float32)
    m_s[...] = m_new


def _paged_attn_body(pt_ref, q_ref, rb_ref, hm_ref, pb_ref, kn_ref, vn_ref, bn_ref, *rest, g, pb_rows, nh, nt):
    k_refs, v_refs = rest[:g], rest[g:2 * g]
    o_ref, m_s, l_s, acc_s = rest[2 * g:]
    j = pl.program_id(1)
    rows = nh * nt
    bf = jnp.bfloat16

    @pl.when(j == 0)
    def _():
        m_s[...] = jnp.full(m_s.shape, RUNNING_MAX_INIT, jnp.float32)
        l_s[...] = jnp.zeros(l_s.shape, jnp.float32)
        acc_s[...] = jnp.zeros(acc_s.shape, jnp.float32)

    q = q_ref[0]
    base = hm_ref[...] + rb_ref[0]
    logits = []
    for gi in range(g):
        kp = k_refs[gi][0]
        lanes = kp.shape[0] * kp.shape[1]
        k2 = kp.reshape(lanes, kp.shape[2]).astype(bf)
        s = lax.dot_general(q, k2, NT_DIMS, preferred_element_type=jnp.float32)
        pb = pb_ref[0, gi]
        if pb_rows != 1:
            pb = jnp.broadcast_to(pb[None], (nh, nt, lanes)).reshape(rows, lanes)
        logits.append(s + base + pb)
    m_old = m_s[...]
    m_new = m_old
    for s in logits:
        m_new = jnp.maximum(m_new, jnp.max(s, axis=-1, keepdims=True))
    alpha = jnp.exp(m_old - m_new)
    l_new = alpha * l_s[...]
    acc_new = alpha * acc_s[...]
    for gi in range(g):
        vp = v_refs[gi][0]
        v2 = vp.reshape(vp.shape[0] * vp.shape[1], vp.shape[2]).astype(bf)
        p = jnp.exp(logits[gi] - m_new)
        l_new = l_new + jnp.sum(p, axis=-1, keepdims=True)
        acc_new = acc_new + jnp.dot(p.astype(bf), v2, preferred_element_type=jnp.float32)
    m_s[...] = m_new
    l_s[...] = l_new
    acc_s[...] = acc_new

    @pl.when(j == pl.num_programs(1) - 1)
    def _():
        s = lax.dot_general(q, kn_ref[0], NT_DIMS, preferred_element_type=jnp.float32) + bn_ref[0]
        _softmax_rows_update(s, vn_ref[0], m_s, l_s, acc_s)
        o_ref[0] = acc_s[...] / l_s[...]


def _head_match_bias(nh, nt, slots):
    row_h = jnp.arange(nh * nt) // nt
    lane_h = jnp.arange(slots * nh) % nh
    return jnp.where(row_h[:, None] == lane_h[None, :], 0.0, MASKED_SCORE).astype(jnp.float32)


def paged_attention(q, k_new, v_new, cache_k, cache_v, page_table, row_bias, page_bias, new_bias):
    B, T, H, D = q.shape
    n_pages = page_table.shape[1]
    slots = cache_k.shape[1]
    lanes = slots * H
    g = PAGES_PER_STEP
    pb_rows = page_bias.shape[2]
    bf = jnp.bfloat16
    qr = (_head_major(q) * HEAD_DIM ** -0.5).reshape(B, H * T, D).astype(bf)
    kn = k_new.reshape(B, T * H, D).astype(bf)
    vn = v_new.reshape(B, T * H, D).astype(bf)
    hm = _head_match_bias(H, T, slots)

    def page_map(gi):
        return lambda b, j, pt: (pt[b, j * g + gi], 0, 0, 0)

    per_batch = lambda b, j, pt: (b, 0, 0)
    page_spec = [pl.BlockSpec((1, slots, H, D), page_map(gi)) for gi in range(g)]
    grid_spec = pltpu.PrefetchScalarGridSpec(
        num_scalar_prefetch=1,
        grid=(B, n_pages // g),
        in_specs=[
            pl.BlockSpec((1, H * T, D), per_batch),
            pl.BlockSpec((1, H * T, 1), per_batch),
            pl.BlockSpec((H * T, lanes), lambda b, j, pt: (0, 0)),
            pl.BlockSpec((1, g, pb_rows, lanes), lambda b, j, pt: (b, j, 0, 0)),
            pl.BlockSpec((1, T * H, D), per_batch),
            pl.BlockSpec((1, T * H, D), per_batch),
            pl.BlockSpec((1, H * T, T * H), per_batch),
        ] + page_spec + page_spec,
        out_specs=pl.BlockSpec((1, H * T, D), per_batch),
        scratch_shapes=[
            pltpu.VMEM((H * T, 1), jnp.float32),
            pltpu.VMEM((H * T, 1), jnp.float32),
            pltpu.VMEM((H * T, D), jnp.float32),
        ],
    )
    o = pl.pallas_call(
        functools.partial(_paged_attn_body, g=g, pb_rows=pb_rows, nh=H, nt=T),
        grid_spec=grid_spec,
        out_shape=jax.ShapeDtypeStruct((B, H * T, D), jnp.float32),
        compiler_params=pltpu.CompilerParams(
            dimension_semantics=("arbitrary", "arbitrary"), vmem_limit_bytes=48 * 1024 * 1024),
        name="paged_attention",
    )(page_table, qr, row_bias, hm, page_bias, kn, vn, new_bias, *([cache_k] * g), *([cache_v] * g))
    return _head_major(o.reshape(B, H, T, D))


def fox_sample(q, k_new, v_new, logf_new, cache_k, cache_v, cache_logf, page_table):
    B, T = q.shape[:2]
    n_pages = page_table.shape[1]
    P = n_pages * PAGE_SIZE
    logf_past = cache_logf[page_table].reshape(B, P, H_FOX).astype(jnp.float32)
    r = jnp.cumsum(logf_past[:, ::-1], axis=1)[:, ::-1] - logf_past
    cn = jnp.cumsum(logf_new, axis=1)
    row_bias = cn.transpose(0, 2, 1).reshape(B, H_FOX * T, 1)
    page_bias = r.reshape(B, n_pages, 1, PAGE_SIZE * H_FOX)
    d = cn.transpose(0, 2, 1)[:, :, :, None] - cn.transpose(0, 2, 1)[:, :, None, :]
    tpos = jnp.arange(T)
    d = jnp.where(tpos[None, :] <= tpos[:, None], d, MASKED_SCORE)
    same = jnp.arange(H_FOX)[:, None] == jnp.arange(H_FOX)[None, :]
    nb = jnp.where(same[None, :, None, None, :], d[:, :, :, :, None], MASKED_SCORE)
    new_bias = nb.reshape(B, H_FOX * T, T * H_FOX)
    return paged_attention(q, k_new, v_new, cache_k, cache_v, page_table, row_bias, page_bias, new_bias)


def swiglu(x, wg, wu, wd):
    return (jax.nn.silu(x @ wg) * (x @ wu)) @ wd


def route(x2, w_router, router_bias):
    n = x2.shape[0]
    scores = jax.nn.sigmoid(jnp.einsum('nd,de->ne', x2, w_router).astype(jnp.float32))
    biased = scores + router_bias.astype(jnp.float32)
    grp = biased.reshape(n, N_GROUPS, N_EXPERTS // N_GROUPS)
    grp_score = lax.top_k(grp, 2)[0].sum(-1)
    _, top_g = lax.top_k(grp_score, TOPK_GROUPS)
    keep_g = (top_g[:, :, None] == jnp.arange(N_GROUPS)[None, None, :]).any(1)
    masked = jnp.where(keep_g[:, :, None], grp, -jnp.inf).reshape(n, N_EXPERTS)
    _, idx = lax.top_k(masked, TOP_K)
    w = jnp.take_along_axis(scores, idx, axis=-1)
    w = w / w.sum(-1, keepdims=True) * ROUTED_SCALE
    return idx, w


RANK_TILE = 256


def _rank_body(idx_ref, rank_ref, count_ref, running):
    i = pl.program_id(0)
    tm, nk = idx_ref.shape

    @pl.when(i == 0)
    def _():
        running[...] = jnp.zeros(running.shape, jnp.float32)

    idx = idx_ref[...]
    lane = lax.broadcasted_iota(jnp.int32, (tm, N_EXPERTS), 1)
    hots = [jnp.where(lane == idx[:, k:k + 1], 1.0, 0.0) for k in range(nk)]
    onehot = hots[0]
    for k in range(1, nk):
        onehot = onehot + hots[k]
    r = lax.broadcasted_iota(jnp.int32, (tm, tm), 0)
    c = lax.broadcasted_iota(jnp.int32, (tm, tm), 1)
    earlier = jnp.where(c < r, 1.0, 0.0).astype(jnp.bfloat16)
    before = jnp.dot(earlier, onehot.astype(jnp.bfloat16), preferred_element_type=jnp.float32) + running[...]
    lane_k = lax.broadcasted_iota(jnp.int32, (tm, nk), 1)
    out = jnp.zeros((tm, nk), jnp.float32)
    for k in range(nk):
        out = jnp.where(lane_k == k, jnp.sum(hots[k] * before, axis=-1, keepdims=True), out)
    rank_ref[...] = out.astype(jnp.int32)
    running[...] = running[...] + jnp.sum(onehot, axis=0, keepdims=True)
    count_ref[...] = running[...]


def expert_rank(idx):
    n, nk = idx.shape
    tm = RANK_TILE
    rank, counts = pl.pallas_call(
        _rank_body,
        grid=(n // tm,),
        in_specs=[pl.BlockSpec((tm, nk), lambda i: (i, 0))],
        out_specs=[pl.BlockSpec((tm, nk), lambda i: (i, 0)),
                   pl.BlockSpec((1, N_EXPERTS), lambda i: (0, 0))],
        out_shape=[jax.ShapeDtypeStruct((n, nk), jnp.int32),
                   jax.ShapeDtypeStruct((1, N_EXPERTS), jnp.float32)],
        scratch_shapes=[pltpu.VMEM((1, N_EXPERTS), jnp.float32)],
        compiler_params=pltpu.CompilerParams(dimension_semantics=("arbitrary",)),
        name="expert_rank",
    )(idx)
    return rank, counts[0].astype(jnp.int32)


def _expert_body(be_ref, x_ref, g_ref, wg_ref, wu_ref, wd_ref, y_ref):
    bf = jnp.bfloat16
    x = x_ref[...].astype(bf)
    a = jnp.dot(x, wg_ref[0].astype(bf), preferred_element_type=jnp.float32)
    u = jnp.dot(x, wu_ref[0].astype(bf), preferred_element_type=jnp.float32)
    hmid = (a / (1.0 + jnp.exp(-a))) * u
    y = jnp.dot(hmid.astype(bf), wd_ref[0].astype(bf), preferred_element_type=jnp.float32)
    y_ref[...] = y * g_ref[...]


def expert_ffn(xb, row_gate, block_expert, w_gate_e, w_up_e, w_down_e):
    rows, d = xb.shape
    tb = EXPERT_BLOCK
    de = w_gate_e.shape[2]
    grid_spec = pltpu.PrefetchScalarGridSpec(
        num_scalar_prefetch=1,
        grid=(rows // tb,),
        in_specs=[
            pl.BlockSpec((tb, d), lambda i, be: (i, 0)),
            pl.BlockSpec((tb, 1), lambda i, be: (i, 0)),
            pl.BlockSpec((1, d, de), lambda i, be: (be[i], 0, 0)),
            pl.BlockSpec((1, d, de), lambda i, be: (be[i], 0, 0)),
            pl.BlockSpec((1, de, d), lambda i, be: (be[i], 0, 0)),
        ],
        out_specs=pl.BlockSpec((tb, d), lambda i, be: (i, 0)),
    )
    return pl.pallas_call(
        _expert_body,
        grid_spec=grid_spec,
        out_shape=jax.ShapeDtypeStruct((rows, d), jnp.float32),
        compiler_params=pltpu.CompilerParams(
            dimension_semantics=("arbitrary",), vmem_limit_bytes=48 * 1024 * 1024),
        name="expert_ffn",
    )(block_expert, xb, row_gate[:, None], w_gate_e, w_up_e, w_down_e)


def routed_experts(x2, idx, gate, w_gate_e, w_up_e, w_down_e):
    n, d = x2.shape
    a = n * TOP_K
    n_blocks = (a + N_EXPERTS * (EXPERT_BLOCK - 1) + EXPERT_BLOCK - 1) // EXPERT_BLOCK
    rows = n_blocks * EXPERT_BLOCK
    rank, counts = expert_rank(idx.astype(jnp.int32))
    padded = (counts + EXPERT_BLOCK - 1) // EXPERT_BLOCK * EXPERT_BLOCK
    pad_end = jnp.cumsum(padded)
    pad_start = pad_end - padded
    dest = (pad_start[idx] + rank).reshape(-1)
    token = jnp.repeat(jnp.arange(n, dtype=jnp.int32), TOP_K)
    row_token = jnp.full((rows,), n, jnp.int32).at[dest].set(token)
    row_gate = jnp.zeros((rows,), jnp.float32).at[dest].set(gate.reshape(-1))
    block_expert = jnp.minimum(
        jnp.searchsorted(pad_end, jnp.arange(n_blocks) * EXPERT_BLOCK, side='right'), N_EXPERTS - 1)
    x_pad = jnp.concatenate([x2, jnp.zeros((1, d), x2.dtype)], axis=0)
    xb = x_pad[row_token]
    yb = expert_ffn(xb, row_gate, block_expert.astype(jnp.int32), w_gate_e, w_up_e, w_down_e)
    return jax.ops.segment_sum(yb, row_token, num_segments=n + 1)[:n]


def moe_ffn(x, w_router, router_bias, w_gate_e, w_up_e, w_down_e, w_gate_s, w_up_s, w_down_s):
    B, T, d = x.shape
    x2 = x.reshape(B * T, d)
    idx, gate = route(x2, w_router, router_bias)
    y = routed_experts(x2, idx, gate, w_gate_e, w_up_e, w_down_e) + swiglu(x2, w_gate_s, w_up_s, w_down_s)
    return y.reshape(B, T, d).astype(x.dtype)


def mix_heads(x, o_a, o_f, w_o, ln1_g, ln1_b):
    B, T, _ = x.shape
    heads = jnp.concatenate([o_a.reshape(B, T, W_DSA), o_f.reshape(B, T, W_FOX)], axis=-1).astype(x.dtype)
    mix = jnp.einsum('btc,cd->btd', heads, w_o)
    return layer_norm(DEEPNORM_ALPHA * x + mix, ln1_g, ln1_b).reshape(B * T, D_MODEL)


def kernel(x_prompt, x_sample, cache_k_dsa, cache_v_dsa, cache_k_idx, cache_k_fox, cache_v_fox,
           cache_logf_fox, page_table, w_in, b_forget, idx_ln_g, idx_ln_b, w_o, ln1_g, ln1_b,
           w_router, router_bias, w_gate_e, w_up_e, w_down_e, w_gate_s, w_up_s, w_down_s,
           ln2_g, ln2_b):
    seq = x_prompt.shape[1]
    dec_seq = x_sample.shape[1]
    past = page_table.shape[1] * PAGE_SIZE
    pos_p = jnp.arange(seq, dtype=jnp.int32)
    pos_s = past + jnp.arange(dec_seq, dtype=jnp.int32)
    xp, xs = x_prompt, x_sample
    l = 0
    q_a, k_a, v_a, q_i, k_i, w_i, q_f, k_f, v_f, logf = project_mixers(
        xp, pos_p, w_in[l], b_forget[l], idx_ln_g[l], idx_ln_b[l])
    o_a = dsa_prompt(q_a, k_a, v_a, q_i, k_i, w_i)
    o_f = fox_prompt(q_f, k_f, v_f, logf)
    hp = mix_heads(xp, o_a, o_f, w_o[l], ln1_g[l], ln1_b[l])
    outs_p = (k_a, v_a, k_i, k_f, v_f, logf.astype(x_prompt.dtype))
    q_a, k_a, v_a, q_i, k_i, w_i, q_f, k_f, v_f, logf = project_mixers(
        xs, pos_s, w_in[l], b_forget[l], idx_ln_g[l], idx_ln_b[l])
    o_a = dsa_sample(q_a, k_a, v_a, q_i, k_i, w_i, cache_k_dsa[l], cache_v_dsa[l], cache_k_idx[l], page_table)
    o_f = fox_sample(q_f, k_f, v_f, logf, cache_k_fox[l], cache_v_fox[l], cache_logf_fox[l], page_table)
    hs = mix_heads(xs, o_a, o_f, w_o[l], ln1_g[l], ln1_b[l])
    outs_s = (k_a, v_a, k_i, k_f, v_f, logf.astype(x_sample.dtype))
    h = jnp.concatenate([hp, hs], axis=0)
    f = moe_ffn(h[None], w_router[l], router_bias[l], w_gate_e[l], w_up_e[l], w_down_e[l],
                w_gate_s[l], w_up_s[l], w_down_s[l])[0]
    y = layer_norm(DEEPNORM_ALPHA * h + f, ln2_g[l], ln2_b[l])
    yp = y[:hp.shape[0]].reshape(x_prompt.shape)
    ys = y[hp.shape[0]:].reshape(x_sample.shape)
    return (yp, ys) + tuple(o[None] for o in outs_p) + tuple(o[None] for o in outs_s)
```

```python
import functools
import math
import jax
import jax.numpy as jnp
from jax import lax
import numpy as np
from jax.experimental import pallas as pl
from jax.experimental.pallas import tpu as pltpu

D_MODEL = 1024
PAGE_SIZE = 128
HEAD_DIM = 64
N_HEADS = D_MODEL // HEAD_DIM
H_DSA = N_HEADS // 2
H_FOX = N_HEADS - H_DSA
W_DSA = H_DSA * HEAD_DIM
W_FOX = H_FOX * HEAD_DIM
MIX_WIDTH = W_DSA + W_FOX
H_IDX = 8
D_IDX = 64
TOPK_MAX = 256
Q_BLOCK = 128
ROPE_THETA = 10000.0
N_EXPERTS = 256
TOP_K = 8
N_GROUPS = 8
TOPK_GROUPS = 4
D_EXPERT = 256
D_SHARED = 256
ROUTED_SCALE = 2.5
EXPERT_BLOCK = 128
LN_EPS = 1e-5
DEPTH = 1
DEEPNORM_ALPHA = (2 * DEPTH) ** 0.25

_IN_SIZES = (W_DSA, W_DSA, W_DSA, H_IDX * D_IDX, D_IDX, H_IDX, W_FOX, W_FOX, W_FOX, H_FOX)
IN_COLS = sum(_IN_SIZES)
SPLIT_POINTS = tuple(int(v) for v in np.cumsum(_IN_SIZES)[:-1])

LANES = 128


def _proj_body(x_ref, w_ref, o_ref):
    o_ref[...] = jnp.dot(x_ref[...].astype(jnp.bfloat16), w_ref[...],
                         preferred_element_type=jnp.float32)


def _project(x2, w):
    m, d = x2.shape
    c = w.shape[1]
    cp = (c + LANES - 1) // LANES * LANES
    wp = jnp.pad(w, ((0, 0), (0, cp - c))).astype(jnp.bfloat16)
    tm = 256
    out = pl.pallas_call(
        _proj_body,
        grid=(m // tm,),
        in_specs=[pl.BlockSpec((tm, d), lambda i: (i, 0)),
                  pl.BlockSpec((d, cp), lambda i: (0, 0))],
        out_specs=pl.BlockSpec((tm, cp), lambda i: (i, 0)),
        out_shape=jax.ShapeDtypeStruct((m, cp), jnp.float32),
        compiler_params=pltpu.CompilerParams(vmem_limit_bytes=48 * 1024 * 1024),
        name="in_proj",
    )(x2, wp)
    return out[:, :c]


def layer_norm(x, g, b):
    xf = x.astype(jnp.float32)
    mu = xf.mean(-1, keepdims=True)
    var = jnp.square(xf - mu).mean(-1, keepdims=True)
    return ((xf - mu) * lax.rsqrt(var + LN_EPS) * g.astype(jnp.float32) + b.astype(jnp.float32)).astype(x.dtype)


def rope(x, pos):
    half = x.shape[-1] // 2
    freqs = ROPE_THETA ** (-jnp.arange(half, dtype=jnp.float32) / half)
    ang = pos.astype(jnp.float32)[:, None] * freqs[None, :]
    cos = jnp.cos(ang)[None, :, None, :]
    sin = jnp.sin(ang)[None, :, None, :]
    xf = x.astype(jnp.float32)
    x1, x2 = xf[..., :half], xf[..., half:]
    return jnp.concatenate([x1 * cos - x2 * sin, x2 * cos + x1 * sin], axis=-1).astype(x.dtype)


def project_mixers(x, pos, w_in, b_forget, idx_ln_g, idx_ln_b):
    B, T, _ = x.shape
    h = _project(x.reshape(B * T, D_MODEL), w_in).reshape(B, T, IN_COLS)
    q_a, k_a, v_a, q_i, k_i, w_i, q_f, k_f, v_f, f_logit = jnp.split(h, SPLIT_POINTS, axis=-1)
    q_a = rope(q_a.reshape(B, T, H_DSA, HEAD_DIM), pos)
    k_a = rope(k_a.reshape(B, T, H_DSA, HEAD_DIM), pos)
    v_a = v_a.reshape(B, T, H_DSA, HEAD_DIM)
    q_i = rope(q_i.reshape(B, T, H_IDX, D_IDX), pos)
    k_i = rope(layer_norm(k_i, idx_ln_g, idx_ln_b)[:, :, None, :], pos)[:, :, 0, :]
    w_i = w_i.astype(jnp.float32) * (H_IDX ** -0.5 * D_IDX ** -0.5)
    q_f = q_f.reshape(B, T, H_FOX, HEAD_DIM)
    k_f = k_f.reshape(B, T, H_FOX, HEAD_DIM)
    v_f = v_f.reshape(B, T, H_FOX, HEAD_DIM)
    logf = jax.nn.log_sigmoid((f_logit + b_forget).astype(jnp.float32))
    return q_a, k_a, v_a, q_i, k_i, w_i, q_f, k_f, v_f, logf


def indexer_scores(q_i, k_i, w_i):
    s = jnp.einsum('bqhd,bld->bqhl', q_i, k_i).astype(jnp.float32)
    return jnp.einsum('bqhl,bqh->bql', jax.nn.relu(s), w_i)


def select_topk(scores, mask, k):
    masked = jnp.where(mask, scores, -jnp.inf)
    vals, idx = lax.top_k(masked, k)
    return idx, vals > -jnp.inf


def gather_rows(src, idx):
    return jax.vmap(lambda s, i: s[i])(src, idx)


def sparse_attend(q, kg, vg, valid):
    s = jnp.einsum('bqhd,bqkhd->bqhk', q, kg).astype(jnp.float32) * HEAD_DIM ** -0.5
    s = jnp.where(valid[:, :, None, :], s, -jnp.inf)
    p = jax.nn.softmax(s, axis=-1)
    return jnp.einsum('bqhk,bqkhd->bqhd', p, vg)


INT_MIN = -2 ** 31
MASKED_SCORE = -1e30
RUNNING_MAX_INIT = -1e29
NT_DIMS = (((1,), (1,)), ((), ()))


def _head_major(x):
    return x.transpose(0, 2, 1, 3)


def _with_ones(v):
    return jnp.concatenate([v, jnp.ones_like(v)], axis=-1)


def _online_softmax_step(s, vc, m_ref, acc_ref):
    m_old = m_ref[...]
    m_new = jnp.maximum(m_old, jnp.max(s, axis=-1, keepdims=True))
    alpha = jnp.exp(m_old - m_new)
    p = jnp.exp(s - m_new)
    pv = jnp.einsum('hqk,hkd->hqd', p.astype(jnp.bfloat16), vc, preferred_element_type=jnp.float32)
    acc_ref[...] = alpha * acc_ref[...] + pv
    m_ref[...] = m_new


def _init_softmax_state(m_ref, acc_ref):
    m_ref[...] = jnp.full(m_ref.shape, RUNNING_MAX_INIT, jnp.float32)
    acc_ref[...] = jnp.zeros(acc_ref.shape, jnp.float32)


def _write_normalised(acc_ref, o_ref):
    a = acc_ref[...]
    o_ref[0] = a[:, :, :HEAD_DIM] / a[:, :, HEAD_DIM:]


def _kth_largest_key(count_ge, rows, kf, total):
    c0 = count_ge(jnp.zeros((rows, 1), jnp.int32))
    nonneg = c0 >= kf
    prefix = jnp.where(nonneg, 0, INT_MIN).astype(jnp.int32)
    cnt = jnp.where(nonneg, c0, total)

    def bit_step(t, carry):
        prefix, cnt = carry
        cand = prefix + lax.shift_left(jnp.int32(1), 30 - t)
        cc = count_ge(cand)
        take = cc >= kf
        return jnp.where(take, cand, prefix), jnp.where(take, cc, cnt)

    thr, cnt = lax.fori_loop(0, 31, bit_step, (prefix, cnt))
    thr = jnp.maximum(thr, INT_MIN + 1)
    excess = cnt > kf
    any_excess = jnp.max(jnp.where(excess, 1.0, 0.0)) > 0.0
    return thr, excess, any_excess


def _last_kept_tie(count_ties_before, need, rows, index_bits):
    def idx_step(t, last):
        cand = last + lax.shift_left(jnp.int32(1), index_bits - 1 - t)
        return jnp.where(count_ties_before(cand) < need, cand, last)

    return lax.fori_loop(0, index_bits, idx_step, jnp.zeros((rows, 1), jnp.int32))


def _dsa_prompt_body(qi_ref, ki_ref, wi_ref, q_ref, k_ref, v_ref, o_ref,
                     keys, wb, m_s, acc_s, *, topk, tq, seq_bits):
    i = pl.program_id(1)
    n_chunks = i + 1
    row = lax.broadcasted_iota(jnp.int32, (tq, tq), 0)
    col = lax.broadcasted_iota(jnp.int32, (tq, tq), 1)
    kf = float(topk)

    w = wi_ref[0]
    for h in range(H_IDX):
        wb[h] = jnp.broadcast_to(w[:, h:h + 1], (tq, tq))
    qi_all = qi_ref[0].reshape(H_IDX * tq, D_IDX)

    def index_chunk(c, carry):
        kic = ki_ref[0, pl.ds(pl.multiple_of(c * tq, tq), tq), :]
        s = lax.dot_general(qi_all, kic, NT_DIMS, preferred_element_type=jnp.float32)
        sc = jnp.maximum(s[0:tq], 0.0) * wb[0]
        for h in range(1, H_IDX):
            sc = sc + jnp.maximum(s[h * tq:(h + 1) * tq], 0.0) * wb[h]
        sc = jnp.where(sc == 0.0, 0.0, sc)
        bits = pltpu.bitcast(sc, jnp.int32)
        keys[c] = jnp.where(bits < 0, bits ^ 0x7FFFFFFF, bits)
        return carry

    lax.fori_loop(0, n_chunks, index_chunk, 0)
    keys[i] = jnp.where(col <= row, keys[i], INT_MIN)

    def count_ge(cand):
        cb = jnp.broadcast_to(cand, (tq, tq))

        def body(c, acc):
            return acc + jnp.where(keys[c] >= cb, 1.0, 0.0)

        acc = lax.fori_loop(0, n_chunks, body, jnp.zeros((tq, tq), jnp.float32))
        return jnp.sum(acc, axis=-1, keepdims=True)

    thr, excess, any_excess = _kth_largest_key(count_ge, tq, kf, (n_chunks * tq).astype(jnp.float32))

    @pl.when(any_excess)
    def _():
        need = kf - count_ge(thr + 1)
        thb = jnp.broadcast_to(thr, (tq, tq))

        def count_ties_before(j):
            jb = jnp.broadcast_to(j, (tq, tq))

            def body(c, acc):
                hit = (keys[c] == thb) & ((col + c * tq) < jb)
                return acc + jnp.where(hit, 1.0, 0.0)

            acc = lax.fori_loop(0, n_chunks, body, jnp.zeros((tq, tq), jnp.float32))
            return jnp.sum(acc, axis=-1, keepdims=True)

        last = _last_kept_tie(count_ties_before, need, tq, seq_bits)
        lb = jnp.broadcast_to(last, (tq, tq))
        eb = jnp.broadcast_to(excess, (tq, tq))

        def drop(c, carry):
            kk = keys[c]
            keys[c] = jnp.where(eb & (kk == thb) & ((col + c * tq) > lb), INT_MIN, kk)
            return carry

        lax.fori_loop(0, n_chunks, drop, 0)

    thb = jnp.broadcast_to(thr, (tq, tq))
    _init_softmax_state(m_s, acc_s)
    qh = q_ref[0]

    def attend_chunk(c, carry):
        sel = keys[c] >= thb
        start = pl.multiple_of(c * tq, tq)
        kc = k_ref[0, :, pl.ds(start, tq), :]
        vc = v_ref[0, :, pl.ds(start, tq), :]
        s = jnp.einsum('hqd,hkd->hqk', qh, kc, preferred_element_type=jnp.float32)
        s = jnp.where(sel[None], s, MASKED_SCORE)
        _online_softmax_step(s, vc, m_s, acc_s)
        return carry

    lax.fori_loop(0, n_chunks, attend_chunk, 0)
    _write_normalised(acc_s, o_ref)


def dsa_prompt(q, k, v, q_i, k_i, w_i):
    B, S = q.shape[:2]
    topk = min(TOPK_MAX, S // 4)
    tq = Q_BLOCK
    nq = S // tq
    bf = jnp.bfloat16
    qh = _head_major(q * HEAD_DIM ** -0.5).astype(bf)
    kh = _head_major(k).astype(bf)
    vh = _with_ones(_head_major(v)).astype(bf)
    qih = _head_major(q_i).astype(bf)
    body = functools.partial(_dsa_prompt_body, topk=topk, tq=tq, seq_bits=max(1, (S - 1).bit_length()))
    o = pl.pallas_call(
        body,
        grid=(B, nq),
        in_specs=[
            pl.BlockSpec((1, H_IDX, tq, D_IDX), lambda b, i: (b, 0, i, 0)),
            pl.BlockSpec((1, S, D_IDX), lambda b, i: (b, 0, 0)),
            pl.BlockSpec((1, tq, H_IDX), lambda b, i: (b, i, 0)),
            pl.BlockSpec((1, H_DSA, tq, HEAD_DIM), lambda b, i: (b, 0, i, 0)),
            pl.BlockSpec((1, H_DSA, S, HEAD_DIM), lambda b, i: (b, 0, 0, 0)),
            pl.BlockSpec((1, H_DSA, S, 2 * HEAD_DIM), lambda b, i: (b, 0, 0, 0)),
        ],
        out_specs=pl.BlockSpec((1, H_DSA, tq, HEAD_DIM), lambda b, i: (b, 0, i, 0)),
        out_shape=jax.ShapeDtypeStruct((B, H_DSA, S, HEAD_DIM), jnp.float32),
        scratch_shapes=[
            pltpu.VMEM((nq, tq, tq), jnp.int32),
            pltpu.VMEM((H_IDX, tq, tq), jnp.float32),
            pltpu.VMEM((H_DSA, tq, tq), jnp.float32),
            pltpu.VMEM((H_DSA, tq, 2 * HEAD_DIM), jnp.float32),
        ],
        compiler_params=pltpu.CompilerParams(
            dimension_semantics=("arbitrary", "arbitrary"), vmem_limit_bytes=48 * 1024 * 1024),
        name="dsa_prompt",
    )(qih, k_i.astype(bf), w_i, qh, kh, vh)
    return _head_major(o)


def _score_keys(s, wcol, nt):
    sc = jnp.maximum(s[0:nt], 0.0) * wcol[0:nt]
    for h in range(1, H_IDX):
        sc = sc + jnp.maximum(s[h * nt:(h + 1) * nt], 0.0) * wcol[h * nt:(h + 1) * nt]
    sc = jnp.where(sc == 0.0, 0.0, sc)
    bits = pltpu.bitcast(sc, jnp.int32)
    return jnp.where(bits < 0, bits ^ 0x7FFFFFFF, bits)


def _idx_sample_body(pt_ref, qi_ref, wc_ref, kin_ref, *rest, g, n_pages, nt, topk, index_bits):
    ki_refs = rest[:g]
    pb_ref, sn_ref, keys = rest[g:]
    j = pl.program_id(1)
    bf = jnp.bfloat16
    slots = keys.shape[2]
    qi = qi_ref[0]
    wcol = wc_ref[0]
    for gi in range(g):
        s = jnp.dot(qi, ki_refs[gi][0].astype(bf), preferred_element_type=jnp.float32)
        keys[j * g + gi] = _score_keys(s, wcol, nt)

    @pl.when(j == pl.num_programs(1) - 1)
    def _():
        row = lax.broadcasted_iota(jnp.int32, (nt, slots), 0)
        col = lax.broadcasted_iota(jnp.int32, (nt, slots), 1)
        s = lax.dot_general(qi, kin_ref[0], NT_DIMS, preferred_element_type=jnp.float32)
        keys[n_pages] = jnp.where(col <= row, _score_keys(s, wcol, nt), INT_MIN)
        kf = float(topk)

        def count_ge(cand):
            hit = jnp.where(keys[...] >= cand[None], 1.0, 0.0)
            return jnp.sum(jnp.sum(hit, axis=0), axis=-1, keepdims=True)

        total = jnp.float32((n_pages + 1) * slots)
        thr, excess, any_excess = _kth_largest_key(count_ge, nt, kf, total)

        @pl.when(any_excess)
        def _():
            need = kf - count_ge(thr + 1)
            pos = (lax.broadcasted_iota(jnp.int32, keys.shape, 0) * slots
                   + lax.broadcasted_iota(jnp.int32, keys.shape, 2))

            def count_ties_before(jj):
                hit = jnp.where((keys[...] == thr[None]) & (pos < jj[None]), 1.0, 0.0)
                return jnp.sum(jnp.sum(hit, axis=0), axis=-1, keepdims=True)

            last = _last_kept_tie(count_ties_before, need, nt, index_bits)
            kk = keys[...]
            keys[...] = jnp.where(excess[None] & (kk == thr[None]) & (pos > last[None]), INT_MIN, kk)

        bias = jnp.where(keys[...] >= thr[None], 0.0, MASKED_SCORE)
        sn_ref[0] = bias[n_pages]
        pb_ref[0] = bias[:n_pages]


def dsa_sample(q, k_new, v_new, q_i, k_i_new, w_i, cache_k, cache_v, cache_ki, page_table):
    B, T, H, D = q.shape
    n_pages = page_table.shape[1]
    slots = cache_ki.shape[1]
    L = n_pages * slots + T
    topk = min(TOPK_MAX, L // 4)
    g = IDX_PAGES_PER_STEP
    bf = jnp.bfloat16
    qi = _head_major(q_i).reshape(B, H_IDX * T, D_IDX).astype(bf)
    wcol = w_i.transpose(0, 2, 1).reshape(B, H_IDX * T, 1)
    ki_new = jnp.pad(k_i_new, ((0, 0), (0, slots - T), (0, 0))).astype(bf)
    cache_kit = _slots_minor(cache_ki)

    def page_map(gi):
        return lambda b, j, pt: (pt[b, j * g + gi], 0, 0)

    per_batch = lambda b, j, pt: (b, 0, 0)
    grid_spec = pltpu.PrefetchScalarGridSpec(
        num_scalar_prefetch=1,
        grid=(B, n_pages // g),
        in_specs=[
            pl.BlockSpec((1, H_IDX * T, D_IDX), per_batch),
            pl.BlockSpec((1, H_IDX * T, 1), per_batch),
            pl.BlockSpec((1, slots, D_IDX), per_batch),
        ] + [pl.BlockSpec((1, D_IDX, slots), page_map(gi)) for gi in range(g)],
        out_specs=[
            pl.BlockSpec((1, n_pages, T, slots), lambda b, j, pt: (b, 0, 0, 0)),
            pl.BlockSpec((1, T, slots), per_batch),
        ],
        scratch_shapes=[pltpu.VMEM((n_pages + 1, T, slots), jnp.int32)],
    )
    page_bias, bias_new = pl.pallas_call(
        functools.partial(_idx_sample_body, g=g, n_pages=n_pages, nt=T, topk=topk,
                          index_bits=max(1, ((n_pages + 1) * slots - 1).bit_length())),
        grid_spec=grid_spec,
        out_shape=[jax.ShapeDtypeStruct((B, n_pages, T, slots), jnp.float32),
                   jax.ShapeDtypeStruct((B, T, slots), jnp.float32)],
        compiler_params=pltpu.CompilerParams(
            dimension_semantics=("arbitrary", "arbitrary"), vmem_limit_bytes=48 * 1024 * 1024),
        name="idx_sample",
    )(page_table, qi, wcol, ki_new, *([cache_kit] * g))
    sel = bias_new[:, :, :T] == 0.0
    same = jnp.arange(H)[:, None] == jnp.arange(H)[None, :]
    nb = jnp.where(same[None, :, None, None, :] & sel[:, None, :, :, None], 0.0, MASKED_SCORE)
    new_bias = nb.reshape(B, H * T, T * H).astype(jnp.float32)
    row_bias = jnp.zeros((B, H * T, 1), jnp.float32)
    return paged_attention(q, k_new, v_new, _slots_minor(cache_k), _slots_minor(cache_v), page_table,
                           row_bias, page_bias, "token", new_bias)


def _fox_prompt_body(q_ref, k_ref, v_ref, ccol_ref, crow_ref, o_ref, cb, m_s, acc_s, *, tq):
    i = pl.program_id(1)
    row = lax.broadcasted_iota(jnp.int32, (tq, tq), 0)
    col = lax.broadcasted_iota(jnp.int32, (tq, tq), 1)
    ccol = ccol_ref[0]
    for h in range(H_FOX):
        cb[h] = jnp.broadcast_to(ccol[:, h:h + 1], (tq, tq))
    _init_softmax_state(m_s, acc_s)
    qh = q_ref[0]

    def attend_chunk(c, diagonal):
        start = pl.multiple_of(c * tq, tq)
        crow = crow_ref[0, :, pl.ds(start, tq)]
        kc = k_ref[0, :, pl.ds(start, tq), :]
        vc = v_ref[0, :, pl.ds(start, tq), :]
        s = jnp.einsum('hqd,hkd->hqk', qh, kc, preferred_element_type=jnp.float32)
        s = s + cb[...] - crow[:, None, :]
        if diagonal:
            s = jnp.where((col <= row)[None], s, MASKED_SCORE)
        _online_softmax_step(s, vc, m_s, acc_s)

    def full_chunk(c, carry):
        attend_chunk(c, False)
        return carry

    lax.fori_loop(0, i, full_chunk, 0)
    attend_chunk(i, True)
    _write_normalised(acc_s, o_ref)


def fox_prompt(q, k, v, logf):
    B, S = q.shape[:2]
    tq = Q_BLOCK
    bf = jnp.bfloat16
    c = jnp.cumsum(logf, axis=1)
    qh = _head_major(q * HEAD_DIM ** -0.5).astype(bf)
    kh = _head_major(k).astype(bf)
    vh = _with_ones(_head_major(v)).astype(bf)
    o = pl.pallas_call(
        functools.partial(_fox_prompt_body, tq=tq),
        grid=(B, S // tq),
        in_specs=[
            pl.BlockSpec((1, H_FOX, tq, HEAD_DIM), lambda b, i: (b, 0, i, 0)),
            pl.BlockSpec((1, H_FOX, S, HEAD_DIM), lambda b, i: (b, 0, 0, 0)),
            pl.BlockSpec((1, H_FOX, S, 2 * HEAD_DIM), lambda b, i: (b, 0, 0, 0)),
            pl.BlockSpec((1, tq, H_FOX), lambda b, i: (b, i, 0)),
            pl.BlockSpec((1, H_FOX, S), lambda b, i: (b, 0, 0)),
        ],
        out_specs=pl.BlockSpec((1, H_FOX, tq, HEAD_DIM), lambda b, i: (b, 0, i, 0)),
        out_shape=jax.ShapeDtypeStruct((B, H_FOX, S, HEAD_DIM), jnp.float32),
        scratch_shapes=[
            pltpu.VMEM((H_FOX, tq, tq), jnp.float32),
            pltpu.VMEM((H_FOX, tq, tq), jnp.float32),
            pltpu.VMEM((H_FOX, tq, 2 * HEAD_DIM), jnp.float32),
        ],
        compiler_params=pltpu.CompilerParams(
            dimension_semantics=("arbitrary", "arbitrary"), vmem_limit_bytes=48 * 1024 * 1024),
        name="fox_prompt",
    )(qh, kh, vh, c, c.transpose(0, 2, 1))
    return _head_major(o)


PAGES_PER_STEP = 8
IDX_PAGES_PER_STEP = 16


def _paged_attn_body(pt_ref, qbd_ref, q_ref, rb_ref, pb_ref, kn_ref, vn_ref, bn_ref, *rest, g, pb_axis, nh, nt):
    k_refs, v_refs = rest[:g], rest[g:2 * g]
    o_ref, m_s, l_s, acc_s = rest[2 * g:]
    j = pl.program_id(1)
    rows = nh * nt
    bf = jnp.bfloat16

    @pl.when(j == 0)
    def _():
        m_s[...] = jnp.full(m_s.shape, RUNNING_MAX_INIT, jnp.float32)
        l_s[...] = jnp.zeros(l_s.shape, jnp.float32)
        acc_s[...] = jnp.zeros(acc_s.shape, jnp.float32)

    qbd = qbd_ref[0]
    rb = rb_ref[0]
    logits = []
    for gi in range(g):
        kp = k_refs[gi][0]
        slots = kp.shape[2]
        k2 = kp.reshape(kp.shape[0] * kp.shape[1], slots).astype(bf)
        s = jnp.dot(qbd, k2, preferred_element_type=jnp.float32)
        pb = pb_ref[0, gi]
        if pb_axis == "head":
            pb = jnp.broadcast_to(pb[:, None, :], (nh, nt, slots)).reshape(rows, slots)
        else:
            pb = jnp.broadcast_to(pb[None], (nh, nt, slots)).reshape(rows, slots)
        logits.append(s + rb + pb)
    m_old = m_s[...]
    m_new = m_old
    for s in logits:
        m_new = jnp.maximum(m_new, jnp.max(s, axis=-1, keepdims=True))
    alpha = jnp.exp(m_old - m_new)
    l_new = alpha * l_s[...]
    acc_new = alpha * acc_s[...]
    for gi in range(g):
        vp = v_refs[gi][0]
        v2 = vp.reshape(vp.shape[0] * vp.shape[1], vp.shape[2]).astype(bf)
        p = jnp.exp(logits[gi] - m_new)
        l_new = l_new + jnp.sum(p, axis=-1, keepdims=True)
        acc_new = acc_new + lax.dot_general(p.astype(bf), v2, NT_DIMS, preferred_element_type=jnp.float32)
    m_s[...] = m_new
    l_s[...] = l_new
    acc_s[...] = acc_new

    @pl.when(j == pl.num_programs(1) - 1)
    def _():
        s = lax.dot_general(q_ref[0], kn_ref[0], NT_DIMS, preferred_element_type=jnp.float32) + bn_ref[0]
        m_prev = m_s[...]
        m_fin = jnp.maximum(m_prev, jnp.max(s, axis=-1, keepdims=True))
        scale_prev = jnp.exp(m_prev - m_fin)
        p = jnp.exp(s - m_fin)
        l_fin = scale_prev * l_s[...] + jnp.sum(p, axis=-1, keepdims=True)
        acc_fin = scale_prev * acc_s[...] + jnp.dot(p.astype(bf), vn_ref[0], preferred_element_type=jnp.float32)
        o_ref[0] = acc_fin / l_fin


def _slots_minor(cache):
    return jnp.moveaxis(cache, 1, -1)


def _block_diag_heads(x):
    B, R, H, D = x.shape
    eye = jnp.eye(H, dtype=x.dtype)
    return (x[:, :, :, None, :] * eye[None, None, :, :, None]).reshape(B, R, H, H * D)


def paged_attention(q, k_new, v_new, cache_k, cache_v, page_table, row_bias, page_bias, pb_axis, new_bias):
    B, T, H, D = q.shape
    n_pages = page_table.shape[1]
    slots = cache_k.shape[3]
    g = PAGES_PER_STEP
    pb_rows = page_bias.shape[2]
    bf = jnp.bfloat16
    qs = _head_major(q) * HEAD_DIM ** -0.5
    qr = qs.reshape(B, H * T, D).astype(bf)
    qbd = _block_diag_heads(qs.transpose(0, 2, 1, 3)).transpose(0, 2, 1, 3).reshape(B, H * T, H * D).astype(bf)
    kn = k_new.reshape(B, T * H, D).astype(bf)
    vn = _block_diag_heads(v_new).reshape(B, T * H, H * D).astype(bf)

    def page_map(gi):
        return lambda b, j, pt: (pt[b, j * g + gi], 0, 0, 0)

    per_batch = lambda b, j, pt: (b, 0, 0)
    page_spec = [pl.BlockSpec((1, H, D, slots), page_map(gi)) for gi in range(g)]
    grid_spec = pltpu.PrefetchScalarGridSpec(
        num_scalar_prefetch=1,
        grid=(B, n_pages // g),
        in_specs=[
            pl.BlockSpec((1, H * T, H * D), per_batch),
            pl.BlockSpec((1, H * T, D), per_batch),
            pl.BlockSpec((1, H * T, 1), per_batch),
            pl.BlockSpec((1, g, pb_rows, slots), lambda b, j, pt: (b, j, 0, 0)),
            pl.BlockSpec((1, T * H, D), per_batch),
            pl.BlockSpec((1, T * H, H * D), per_batch),
            pl.BlockSpec((1, H * T, T * H), per_batch),
        ] + page_spec + page_spec,
        out_specs=pl.BlockSpec((1, H * T, H * D), per_batch),
        scratch_shapes=[
            pltpu.VMEM((H * T, 1), jnp.float32),
            pltpu.VMEM((H * T, 1), jnp.float32),
            pltpu.VMEM((H * T, H * D), jnp.float32),
        ],
    )
    o = pl.pallas_call(
        functools.partial(_paged_attn_body, g=g, pb_axis=pb_axis, nh=H, nt=T),
        grid_spec=grid_spec,
        out_shape=jax.ShapeDtypeStruct((B, H * T, H * D), jnp.float32),
        compiler_params=pltpu.CompilerParams(
            dimension_semantics=("arbitrary", "arbitrary"), vmem_limit_bytes=48 * 1024 * 1024),
        name="paged_attention",
    )(page_table, qbd, qr, row_bias, page_bias, kn, vn, new_bias, *([cache_k] * g), *([cache_v] * g))
    o = o.reshape(B, H, T, H, D)
    hh = jnp.arange(H)
    return o[:, hh, :, hh, :].transpose(1, 2, 0, 3)


def fox_sample(q, k_new, v_new, logf_new, cache_k, cache_v, cache_logf, page_table):
    B, T = q.shape[:2]
    n_pages = page_table.shape[1]
    P = n_pages * PAGE_SIZE
    logf_past = cache_logf[page_table].reshape(B, P, H_FOX).astype(jnp.float32)
    r = jnp.cumsum(logf_past[:, ::-1], axis=1)[:, ::-1] - logf_past
    cn = jnp.cumsum(logf_new, axis=1)
    row_bias = cn.transpose(0, 2, 1).reshape(B, H_FOX * T, 1)
    page_bias = r.reshape(B, n_pages, PAGE_SIZE, H_FOX).transpose(0, 1, 3, 2)
    d = cn.transpose(0, 2, 1)[:, :, :, None] - cn.transpose(0, 2, 1)[:, :, None, :]
    tpos = jnp.arange(T)
    d = jnp.where(tpos[None, :] <= tpos[:, None], d, MASKED_SCORE)
    same = jnp.arange(H_FOX)[:, None] == jnp.arange(H_FOX)[None, :]
    nb = jnp.where(same[None, :, None, None, :], d[:, :, :, :, None], MASKED_SCORE)
    new_bias = nb.reshape(B, H_FOX * T, T * H_FOX)
    return paged_attention(q, k_new, v_new, _slots_minor(cache_k), _slots_minor(cache_v), page_table,
                           row_bias, page_bias, "head", new_bias)


def swiglu(x, wg, wu, wd):
    return (jax.nn.silu(x @ wg) * (x @ wu)) @ wd


def route(x2, w_router, router_bias):
    n = x2.shape[0]
    scores = jax.nn.sigmoid(jnp.einsum('nd,de->ne', x2, w_router).astype(jnp.float32))
    biased = scores + router_bias.astype(jnp.float32)
    grp = biased.reshape(n, N_GROUPS, N_EXPERTS // N_GROUPS)
    grp_score = lax.top_k(grp, 2)[0].sum(-1)
    _, top_g = lax.top_k(grp_score, TOPK_GROUPS)
    keep_g = (top_g[:, :, None] == jnp.arange(N_GROUPS)[None, None, :]).any(1)
    masked = jnp.where(keep_g[:, :, None], grp, -jnp.inf).reshape(n, N_EXPERTS)
    _, idx = lax.top_k(masked, TOP_K)
    w = jnp.take_along_axis(scores, idx, axis=-1)
    w = w / w.sum(-1, keepdims=True) * ROUTED_SCALE
    return idx, w


RANK_TILE = 256


def _rank_body(idx_ref, rank_ref, count_ref, running):
    i = pl.program_id(0)
    tm, nk = idx_ref.shape

    @pl.when(i == 0)
    def _():
        running[...] = jnp.zeros(running.shape, jnp.float32)

    idx = idx_ref[...]
    lane = lax.broadcasted_iota(jnp.int32, (tm, N_EXPERTS), 1)
    hots = [jnp.where(lane == idx[:, k:k + 1], 1.0, 0.0) for k in range(nk)]
    onehot = hots[0]
    for k in range(1, nk):
        onehot = onehot + hots[k]
    r = lax.broadcasted_iota(jnp.int32, (tm, tm), 0)
    c = lax.broadcasted_iota(jnp.int32, (tm, tm), 1)
    earlier = jnp.where(c < r, 1.0, 0.0).astype(jnp.bfloat16)
    before = jnp.dot(earlier, onehot.astype(jnp.bfloat16), preferred_element_type=jnp.float32) + running[...]
    lane_k = lax.broadcasted_iota(jnp.int32, (tm, nk), 1)
    out = jnp.zeros((tm, nk), jnp.float32)
    for k in range(nk):
        out = jnp.where(lane_k == k, jnp.sum(hots[k] * before, axis=-1, keepdims=True), out)
    rank_ref[...] = out.astype(jnp.int32)
    running[...] = running[...] + jnp.sum(onehot, axis=0, keepdims=True)
    count_ref[...] = running[...]


def expert_rank(idx):
    n, nk = idx.shape
    tm = RANK_TILE
    rank, counts = pl.pallas_call(
        _rank_body,
        grid=(n // tm,),
        in_specs=[pl.BlockSpec((tm, nk), lambda i: (i, 0))],
        out_specs=[pl.BlockSpec((tm, nk), lambda i: (i, 0)),
                   pl.BlockSpec((1, N_EXPERTS), lambda i: (0, 0))],
        out_shape=[jax.ShapeDtypeStruct((n, nk), jnp.int32),
                   jax.ShapeDtypeStruct((1, N_EXPERTS), jnp.float32)],
        scratch_shapes=[pltpu.VMEM((1, N_EXPERTS), jnp.float32)],
        compiler_params=pltpu.CompilerParams(dimension_semantics=("arbitrary",)),
        name="expert_rank",
    )(idx)
    return rank, counts[0].astype(jnp.int32)


EXPERT_TILE = 256


def _expert_body(be_ref, nu_ref, x_ref, g_ref, wg_ref, wu_ref, wd_ref, y_ref):
    i = pl.program_id(0)

    @pl.when(i < nu_ref[0])
    def _():
        bf = jnp.bfloat16
        x = x_ref[...].astype(bf)
        a = jnp.dot(x, wg_ref[0].astype(bf), preferred_element_type=jnp.float32)
        u = jnp.dot(x, wu_ref[0].astype(bf), preferred_element_type=jnp.float32)
        hmid = (a / (1.0 + jnp.exp(-a))) * u
        y = jnp.dot(hmid.astype(bf), wd_ref[0].astype(bf), preferred_element_type=jnp.float32)
        y_ref[...] = y * g_ref[...]

    @pl.when(i >= nu_ref[0])
    def _():
        y_ref[...] = jnp.zeros(y_ref.shape, jnp.float32)


def expert_ffn(xb, row_gate, tile_expert, n_used, w_gate_e, w_up_e, w_down_e):
    rows, d = xb.shape
    tb = EXPERT_TILE
    de = w_gate_e.shape[2]
    grid_spec = pltpu.PrefetchScalarGridSpec(
        num_scalar_prefetch=2,
        grid=(rows // tb,),
        in_specs=[
            pl.BlockSpec((tb, d), lambda i, te, nu: (i, 0)),
            pl.BlockSpec((tb, 1), lambda i, te, nu: (i, 0)),
            pl.BlockSpec((1, d, de), lambda i, te, nu: (te[i], 0, 0)),
            pl.BlockSpec((1, d, de), lambda i, te, nu: (te[i], 0, 0)),
            pl.BlockSpec((1, de, d), lambda i, te, nu: (te[i], 0, 0)),
        ],
        out_specs=pl.BlockSpec((tb, d), lambda i, te, nu: (i, 0)),
    )
    return pl.pallas_call(
        _expert_body,
        grid_spec=grid_spec,
        out_shape=jax.ShapeDtypeStruct((rows, d), jnp.float32),
        compiler_params=pltpu.CompilerParams(
            dimension_semantics=("arbitrary",), vmem_limit_bytes=48 * 1024 * 1024),
        name="expert_ffn",
    )(tile_expert, n_used, xb, row_gate[:, None], w_gate_e, w_up_e, w_down_e)


def routed_experts(x2, idx, gate, w_gate_e, w_up_e, w_down_e):
    n, d = x2.shape
    a = n * TOP_K
    n_tiles = (a + N_EXPERTS * (EXPERT_TILE - 1) + EXPERT_TILE - 1) // EXPERT_TILE
    rows = n_tiles * EXPERT_TILE
    rank, counts = expert_rank(idx.astype(jnp.int32))
    padded = (counts + EXPERT_TILE - 1) // EXPERT_TILE * EXPERT_TILE
    pad_end = jnp.cumsum(padded)
    pad_start = pad_end - padded
    dest2 = pad_start[idx] + rank
    dest = dest2.reshape(-1)
    token = jnp.repeat(jnp.arange(n, dtype=jnp.int32), TOP_K)
    row_token = jnp.full((rows,), n, jnp.int32).at[dest].set(token)
    row_gate = jnp.zeros((rows,), jnp.float32).at[dest].set(gate.reshape(-1))
    tile_expert = jnp.minimum(
        jnp.searchsorted(pad_end, jnp.arange(n_tiles) * EXPERT_TILE, side='right'), N_EXPERTS - 1)
    n_used = (pad_end[-1:] // EXPERT_TILE).astype(jnp.int32)
    x_pad = jnp.concatenate([x2, jnp.zeros((1, d), x2.dtype)], axis=0)
    xb = x_pad[row_token]
    yb = expert_ffn(xb, row_gate, tile_expert.astype(jnp.int32), n_used, w_gate_e, w_up_e, w_down_e)
    return yb[dest2].sum(axis=1)


def moe_ffn(x, w_router, router_bias, w_gate_e, w_up_e, w_down_e, w_gate_s, w_up_s, w_down_s):
    B, T, d = x.shape
    x2 = x.reshape(B * T, d)
    idx, gate = route(x2, w_router, router_bias)
    y = routed_experts(x2, idx, gate, w_gate_e, w_up_e, w_down_e) + swiglu(x2, w_gate_s, w_up_s, w_down_s)
    return y.reshape(B, T, d).astype(x.dtype)


def mix_heads(x, o_a, o_f, w_o, ln1_g, ln1_b):
    B, T, _ = x.shape
    heads = jnp.concatenate([o_a.reshape(B, T, W_DSA), o_f.reshape(B, T, W_FOX)], axis=-1).astype(x.dtype)
    mix = jnp.einsum('btc,cd->btd', heads, w_o)
    return layer_norm(DEEPNORM_ALPHA * x + mix, ln1_g, ln1_b).reshape(B * T, D_MODEL)


def kernel(x_prompt, x_sample, cache_k_dsa, cache_v_dsa, cache_k_idx, cache_k_fox, cache_v_fox,
           cache_logf_fox, page_table, w_in, b_forget, idx_ln_g, idx_ln_b, w_o, ln1_g, ln1_b,
           w_router, router_bias, w_gate_e, w_up_e, w_down_e, w_gate_s, w_up_s, w_down_s,
           ln2_g, ln2_b):
    seq = x_prompt.shape[1]
    dec_seq = x_sample.shape[1]
    past = page_table.shape[1] * PAGE_SIZE
    pos_p = jnp.arange(seq, dtype=jnp.int32)
    pos_s = past + jnp.arange(dec_seq, dtype=jnp.int32)
    xp, xs = x_prompt, x_sample
    l = 0
    q_a, k_a, v_a, q_i, k_i, w_i, q_f, k_f, v_f, logf = project_mixers(
        xp, pos_p, w_in[l], b_forget[l], idx_ln_g[l], idx_ln_b[l])
    o_a = dsa_prompt(q_a, k_a, v_a, q_i, k_i, w_i)
    o_f = fox_prompt(q_f, k_f, v_f, logf)
    hp = mix_heads(xp, o_a, o_f, w_o[l], ln1_g[l], ln1_b[l])
    outs_p = (k_a, v_a, k_i, k_f, v_f, logf.astype(x_prompt.dtype))
    q_a, k_a, v_a, q_i, k_i, w_i, q_f, k_f, v_f, logf = project_mixers(
        xs, pos_s, w_in[l], b_forget[l], idx_ln_g[l], idx_ln_b[l])
    o_a = dsa_sample(q_a, k_a, v_a, q_i, k_i, w_i, cache_k_dsa[l], cache_v_dsa[l], cache_k_idx[l], page_table)
    o_f = fox_sample(q_f, k_f, v_f, logf, cache_k_fox[l], cache_v_fox[l], cache_logf_fox[l], page_table)
    hs = mix_heads(xs, o_a, o_f, w_o[l], ln1_g[l], ln1_b[l])
    outs_s = (k_a, v_a, k_i, k_f, v_f, logf.astype(x_sample.dtype))
    h = jnp.concatenate([hp, hs], axis=0)
    f = moe_ffn(h[None], w_router[l], router_bias[l], w_gate_e[l], w_up_e[l], w_down_e[l],
                w_gate_s[l], w_up_s[l], w_down_s[l])[0]
    y = layer_norm(DEEPNORM_ALPHA * h + f, ln2_g[l], ln2_b[l])
    yp = y[:hp.shape[0]].reshape(x_prompt.shape)
    ys = y[hp.shape[0]:].reshape(x_sample.shape)
    return (yp, ys) + tuple(o[None] for o in outs_p) + tuple(o[None] for o in outs_s)
```

```python
import functools
import math
import jax
import jax.numpy as jnp
from jax import lax
import numpy as np
from jax.experimental import pallas as pl
from jax.experimental.pallas import tpu as pltpu

D_MODEL = 1024
PAGE_SIZE = 128
HEAD_DIM = 64
N_HEADS = D_MODEL // HEAD_DIM
H_DSA = N_HEADS // 2
H_FOX = N_HEADS - H_DSA
W_DSA = H_DSA * HEAD_DIM
W_FOX = H_FOX * HEAD_DIM
MIX_WIDTH = W_DSA + W_FOX
H_IDX = 8
D_IDX = 64
TOPK_MAX = 256
Q_BLOCK = 128
ROPE_THETA = 10000.0
N_EXPERTS = 256
TOP_K = 8
N_GROUPS = 8
TOPK_GROUPS = 4
D_EXPERT = 256
D_SHARED = 256
ROUTED_SCALE = 2.5
EXPERT_BLOCK = 128
LN_EPS = 1e-5
DEPTH = 1
DEEPNORM_ALPHA = (2 * DEPTH) ** 0.25

_IN_SIZES = (W_DSA, W_DSA, W_DSA, H_IDX * D_IDX, D_IDX, H_IDX, W_FOX, W_FOX, W_FOX, H_FOX)
IN_COLS = sum(_IN_SIZES)
SPLIT_POINTS = tuple(int(v) for v in np.cumsum(_IN_SIZES)[:-1])

LANES = 128


def _proj_body(x_ref, w_ref, o_ref):
    o_ref[...] = jnp.dot(x_ref[...].astype(jnp.bfloat16), w_ref[...],
                         preferred_element_type=jnp.float32)


def _project(x2, w):
    m, d = x2.shape
    c = w.shape[1]
    cp = (c + LANES - 1) // LANES * LANES
    wp = jnp.pad(w, ((0, 0), (0, cp - c))).astype(jnp.bfloat16)
    tm = 256
    out = pl.pallas_call(
        _proj_body,
        grid=(m // tm,),
        in_specs=[pl.BlockSpec((tm, d), lambda i: (i, 0)),
                  pl.BlockSpec((d, cp), lambda i: (0, 0))],
        out_specs=pl.BlockSpec((tm, cp), lambda i: (i, 0)),
        out_shape=jax.ShapeDtypeStruct((m, cp), jnp.float32),
        compiler_params=pltpu.CompilerParams(vmem_limit_bytes=48 * 1024 * 1024),
        name="in_proj",
    )(x2, wp)
    return out[:, :c]


def layer_norm(x, g, b):
    xf = x.astype(jnp.float32)
    mu = xf.mean(-1, keepdims=True)
    var = jnp.square(xf - mu).mean(-1, keepdims=True)
    return ((xf - mu) * lax.rsqrt(var + LN_EPS) * g.astype(jnp.float32) + b.astype(jnp.float32)).astype(x.dtype)


def rope(x, pos):
    half = x.shape[-1] // 2
    freqs = ROPE_THETA ** (-jnp.arange(half, dtype=jnp.float32) / half)
    ang = pos.astype(jnp.float32)[:, None] * freqs[None, :]
    cos = jnp.cos(ang)[None, :, None, :]
    sin = jnp.sin(ang)[None, :, None, :]
    xf = x.astype(jnp.float32)
    x1, x2 = xf[..., :half], xf[..., half:]
    return jnp.concatenate([x1 * cos - x2 * sin, x2 * cos + x1 * sin], axis=-1).astype(x.dtype)


def project_mixers(x, pos, w_in, b_forget, idx_ln_g, idx_ln_b):
    B, T, _ = x.shape
    h = _project(x.reshape(B * T, D_MODEL), w_in).reshape(B, T, IN_COLS)
    q_a, k_a, v_a, q_i, k_i, w_i, q_f, k_f, v_f, f_logit = jnp.split(h, SPLIT_POINTS, axis=-1)
    q_a = rope(q_a.reshape(B, T, H_DSA, HEAD_DIM), pos)
    k_a = rope(k_a.reshape(B, T, H_DSA, HEAD_DIM), pos)
    v_a = v_a.reshape(B, T, H_DSA, HEAD_DIM)
    q_i = rope(q_i.reshape(B, T, H_IDX, D_IDX), pos)
    k_i = rope(layer_norm(k_i, idx_ln_g, idx_ln_b)[:, :, None, :], pos)[:, :, 0, :]
    w_i = w_i.astype(jnp.float32) * (H_IDX ** -0.5 * D_IDX ** -0.5)
    q_f = q_f.reshape(B, T, H_FOX, HEAD_DIM)
    k_f = k_f.reshape(B, T, H_FOX, HEAD_DIM)
    v_f = v_f.reshape(B, T, H_FOX, HEAD_DIM)
    logf = jax.nn.log_sigmoid((f_logit + b_forget).astype(jnp.float32))
    return q_a, k_a, v_a, q_i, k_i, w_i, q_f, k_f, v_f, logf


def indexer_scores(q_i, k_i, w_i):
    s = jnp.einsum('bqhd,bld->bqhl', q_i, k_i).astype(jnp.float32)
    return jnp.einsum('bqhl,bqh->bql', jax.nn.relu(s), w_i)


def select_topk(scores, mask, k):
    masked = jnp.where(mask, scores, -jnp.inf)
    vals, idx = lax.top_k(masked, k)
    return idx, vals > -jnp.inf


def gather_rows(src, idx):
    return jax.vmap(lambda s, i: s[i])(src, idx)


def sparse_attend(q, kg, vg, valid):
    s = jnp.einsum('bqhd,bqkhd->bqhk', q, kg).astype(jnp.float32) * HEAD_DIM ** -0.5
    s = jnp.where(valid[:, :, None, :], s, -jnp.inf)
    p = jax.nn.softmax(s, axis=-1)
    return jnp.einsum('bqhk,bqkhd->bqhd', p, vg)


INT_MIN = -2 ** 31
MASKED_SCORE = -1e30
RUNNING_MAX_INIT = -1e29
NT_DIMS = (((1,), (1,)), ((), ()))


def _head_major(x):
    return x.transpose(0, 2, 1, 3)


def _with_ones(v):
    return jnp.concatenate([v, jnp.ones_like(v)], axis=-1)


def _online_softmax_step(s, vc, m_ref, acc_ref):
    m_old = m_ref[...]
    m_new = jnp.maximum(m_old, jnp.max(s, axis=-1, keepdims=True))
    alpha = jnp.exp(m_old - m_new)
    p = jnp.exp(s - m_new)
    pv = jnp.einsum('hqk,hkd->hqd', p.astype(jnp.bfloat16), vc, preferred_element_type=jnp.float32)
    acc_ref[...] = alpha * acc_ref[...] + pv
    m_ref[...] = m_new


def _init_softmax_state(m_ref, acc_ref):
    m_ref[...] = jnp.full(m_ref.shape, RUNNING_MAX_INIT, jnp.float32)
    acc_ref[...] = jnp.zeros(acc_ref.shape, jnp.float32)


def _write_normalised(acc_ref, o_ref):
    a = acc_ref[...]
    o_ref[0] = a[:, :, :HEAD_DIM] / a[:, :, HEAD_DIM:]


def _kth_largest_key(count_ge, rows, kf, total):
    c0 = count_ge(jnp.zeros((rows, 1), jnp.int32))
    nonneg = c0 >= kf
    prefix = jnp.where(nonneg, 0, INT_MIN).astype(jnp.int32)
    cnt = jnp.where(nonneg, c0, total)

    def bit_step(t, carry):
        prefix, cnt = carry
        cand = prefix + lax.shift_left(jnp.int32(1), 30 - t)
        cc = count_ge(cand)
        take = cc >= kf
        return jnp.where(take, cand, prefix), jnp.where(take, cc, cnt)

    thr, cnt = lax.fori_loop(0, 31, bit_step, (prefix, cnt))
    thr = jnp.maximum(thr, INT_MIN + 1)
    excess = cnt > kf
    any_excess = jnp.max(jnp.where(excess, 1.0, 0.0)) > 0.0
    return thr, excess, any_excess


def _last_kept_tie(count_ties_before, need, rows, index_bits):
    def idx_step(t, last):
        cand = last + lax.shift_left(jnp.int32(1), index_bits - 1 - t)
        return jnp.where(count_ties_before(cand) < need, cand, last)

    return lax.fori_loop(0, index_bits, idx_step, jnp.zeros((rows, 1), jnp.int32))


def _dsa_prompt_body(qi_ref, ki_ref, wi_ref, q_ref, k_ref, v_ref, o_ref,
                     keys, wb, m_s, acc_s, *, topk, tq, seq_bits):
    i = pl.program_id(1)
    n_chunks = i + 1
    row = lax.broadcasted_iota(jnp.int32, (tq, tq), 0)
    col = lax.broadcasted_iota(jnp.int32, (tq, tq), 1)
    kf = float(topk)

    w = wi_ref[0]
    for h in range(H_IDX):
        wb[h] = jnp.broadcast_to(w[:, h:h + 1], (tq, tq))
    qi_all = qi_ref[0].reshape(H_IDX * tq, D_IDX)

    def index_chunk(c, carry):
        kic = ki_ref[0, pl.ds(pl.multiple_of(c * tq, tq), tq), :]
        s = lax.dot_general(qi_all, kic, NT_DIMS, preferred_element_type=jnp.float32)
        sc = jnp.maximum(s[0:tq], 0.0) * wb[0]
        for h in range(1, H_IDX):
            sc = sc + jnp.maximum(s[h * tq:(h + 1) * tq], 0.0) * wb[h]
        sc = jnp.where(sc == 0.0, 0.0, sc)
        bits = pltpu.bitcast(sc, jnp.int32)
        keys[c] = jnp.where(bits < 0, bits ^ 0x7FFFFFFF, bits)
        return carry

    lax.fori_loop(0, n_chunks, index_chunk, 0)
    keys[i] = jnp.where(col <= row, keys[i], INT_MIN)

    def count_ge(cand):
        cb = jnp.broadcast_to(cand, (tq, tq))

        def body(c, acc):
            return acc + jnp.where(keys[c] >= cb, 1.0, 0.0)

        acc = lax.fori_loop(0, n_chunks, body, jnp.zeros((tq, tq), jnp.float32))
        return jnp.sum(acc, axis=-1, keepdims=True)

    thr, excess, any_excess = _kth_largest_key(count_ge, tq, kf, (n_chunks * tq).astype(jnp.float32))

    @pl.when(any_excess)
    def _():
        need = kf - count_ge(thr + 1)
        thb = jnp.broadcast_to(thr, (tq, tq))

        def count_ties_before(j):
            jb = jnp.broadcast_to(j, (tq, tq))

            def body(c, acc):
                hit = (keys[c] == thb) & ((col + c * tq) < jb)
                return acc + jnp.where(hit, 1.0, 0.0)

            acc = lax.fori_loop(0, n_chunks, body, jnp.zeros((tq, tq), jnp.float32))
            return jnp.sum(acc, axis=-1, keepdims=True)

        last = _last_kept_tie(count_ties_before, need, tq, seq_bits)
        lb = jnp.broadcast_to(last, (tq, tq))
        eb = jnp.broadcast_to(excess, (tq, tq))

        def drop(c, carry):
            kk = keys[c]
            keys[c] = jnp.where(eb & (kk == thb) & ((col + c * tq) > lb), INT_MIN, kk)
            return carry

        lax.fori_loop(0, n_chunks, drop, 0)

    thb = jnp.broadcast_to(thr, (tq, tq))
    _init_softmax_state(m_s, acc_s)
    qh = q_ref[0]

    def attend_chunk(c, carry):
        sel = keys[c] >= thb
        start = pl.multiple_of(c * tq, tq)
        kc = k_ref[0, :, pl.ds(start, tq), :]
        vc = v_ref[0, :, pl.ds(start, tq), :]
        s = jnp.einsum('hqd,hkd->hqk', qh, kc, preferred_element_type=jnp.float32)
        s = jnp.where(sel[None], s, MASKED_SCORE)
        _online_softmax_step(s, vc, m_s, acc_s)
        return carry

    lax.fori_loop(0, n_chunks, attend_chunk, 0)
    _write_normalised(acc_s, o_ref)


def dsa_prompt(q, k, v, q_i, k_i, w_i):
    B, S = q.shape[:2]
    topk = min(TOPK_MAX, S // 4)
    tq = Q_BLOCK
    nq = S // tq
    bf = jnp.bfloat16
    qh = _head_major(q * HEAD_DIM ** -0.5).astype(bf)
    kh = _head_major(k).astype(bf)
    vh = _with_ones(_head_major(v)).astype(bf)
    qih = _head_major(q_i).astype(bf)
    body = functools.partial(_dsa_prompt_body, topk=topk, tq=tq, seq_bits=max(1, (S - 1).bit_length()))
    o = pl.pallas_call(
        body,
        grid=(B, nq),
        in_specs=[
            pl.BlockSpec((1, H_IDX, tq, D_IDX), lambda b, i: (b, 0, i, 0)),
            pl.BlockSpec((1, S, D_IDX), lambda b, i: (b, 0, 0)),
            pl.BlockSpec((1, tq, H_IDX), lambda b, i: (b, i, 0)),
            pl.BlockSpec((1, H_DSA, tq, HEAD_DIM), lambda b, i: (b, 0, i, 0)),
            pl.BlockSpec((1, H_DSA, S, HEAD_DIM), lambda b, i: (b, 0, 0, 0)),
            pl.BlockSpec((1, H_DSA, S, 2 * HEAD_DIM), lambda b, i: (b, 0, 0, 0)),
        ],
        out_specs=pl.BlockSpec((1, H_DSA, tq, HEAD_DIM), lambda b, i: (b, 0, i, 0)),
        out_shape=jax.ShapeDtypeStruct((B, H_DSA, S, HEAD_DIM), jnp.float32),
        scratch_shapes=[
            pltpu.VMEM((nq, tq, tq), jnp.int32),
            pltpu.VMEM((H_IDX, tq, tq), jnp.float32),
            pltpu.VMEM((H_DSA, tq, tq), jnp.float32),
            pltpu.VMEM((H_DSA, tq, 2 * HEAD_DIM), jnp.float32),
        ],
        compiler_params=pltpu.CompilerParams(
            dimension_semantics=("arbitrary", "arbitrary"), vmem_limit_bytes=48 * 1024 * 1024),
        name="dsa_prompt",
    )(qih, k_i.astype(bf), w_i, qh, kh, vh)
    return _head_major(o)


def _score_keys(s, wcol, nt):
    sc = jnp.maximum(s[0:nt], 0.0) * wcol[0:nt]
    for h in range(1, H_IDX):
        sc = sc + jnp.maximum(s[h * nt:(h + 1) * nt], 0.0) * wcol[h * nt:(h + 1) * nt]
    sc = jnp.where(sc == 0.0, 0.0, sc)
    bits = pltpu.bitcast(sc, jnp.int32)
    return jnp.where(bits < 0, bits ^ 0x7FFFFFFF, bits)


def _idx_sample_body(pt_ref, qi_ref, wc_ref, kin_ref, *rest, g, n_pages, nt, topk, index_bits):
    ki_refs = rest[:g]
    pb_ref, sn_ref, keys = rest[g:]
    j = pl.program_id(1)
    bf = jnp.bfloat16
    slots = keys.shape[2]
    qi = qi_ref[0]
    wcol = wc_ref[0]
    for gi in range(g):
        s = jnp.dot(qi, ki_refs[gi][0].astype(bf), preferred_element_type=jnp.float32)
        keys[j * g + gi] = _score_keys(s, wcol, nt)

    @pl.when(j == pl.num_programs(1) - 1)
    def _():
        row = lax.broadcasted_iota(jnp.int32, (nt, slots), 0)
        col = lax.broadcasted_iota(jnp.int32, (nt, slots), 1)
        s = lax.dot_general(qi, kin_ref[0], NT_DIMS, preferred_element_type=jnp.float32)
        keys[n_pages] = jnp.where(col <= row, _score_keys(s, wcol, nt), INT_MIN)
        kf = float(topk)

        def count_ge(cand):
            hit = jnp.where(keys[...] >= cand[None], 1.0, 0.0)
            return jnp.sum(jnp.sum(hit, axis=0), axis=-1, keepdims=True)

        total = jnp.float32((n_pages + 1) * slots)
        thr, excess, any_excess = _kth_largest_key(count_ge, nt, kf, total)

        @pl.when(any_excess)
        def _():
            need = kf - count_ge(thr + 1)
            pos = (lax.broadcasted_iota(jnp.int32, keys.shape, 0) * slots
                   + lax.broadcasted_iota(jnp.int32, keys.shape, 2))

            def count_ties_before(jj):
                hit = jnp.where((keys[...] == thr[None]) & (pos < jj[None]), 1.0, 0.0)
                return jnp.sum(jnp.sum(hit, axis=0), axis=-1, keepdims=True)

            last = _last_kept_tie(count_ties_before, need, nt, index_bits)
            kk = keys[...]
            keys[...] = jnp.where(excess[None] & (kk == thr[None]) & (pos > last[None]), INT_MIN, kk)

        bias = jnp.where(keys[...] >= thr[None], 0.0, MASKED_SCORE)
        sn_ref[0] = bias[n_pages]
        pb_ref[0] = bias[:n_pages]


def dsa_sample(q, k_new, v_new, q_i, k_i_new, w_i, cache_k, cache_v, cache_ki, page_table):
    B, T, H, D = q.shape
    n_pages = page_table.shape[1]
    slots = cache_ki.shape[1]
    L = n_pages * slots + T
    topk = min(TOPK_MAX, L // 4)
    g = IDX_PAGES_PER_STEP
    bf = jnp.bfloat16
    qi = _head_major(q_i).reshape(B, H_IDX * T, D_IDX).astype(bf)
    wcol = w_i.transpose(0, 2, 1).reshape(B, H_IDX * T, 1)
    ki_new = jnp.pad(k_i_new, ((0, 0), (0, slots - T), (0, 0))).astype(bf)
    cache_kit = _slots_minor(cache_ki)

    def page_map(gi):
        return lambda b, j, pt: (pt[b, j * g + gi], 0, 0)

    per_batch = lambda b, j, pt: (b, 0, 0)
    grid_spec = pltpu.PrefetchScalarGridSpec(
        num_scalar_prefetch=1,
        grid=(B, n_pages // g),
        in_specs=[
            pl.BlockSpec((1, H_IDX * T, D_IDX), per_batch),
            pl.BlockSpec((1, H_IDX * T, 1), per_batch),
            pl.BlockSpec((1, slots, D_IDX), per_batch),
        ] + [pl.BlockSpec((1, D_IDX, slots), page_map(gi)) for gi in range(g)],
        out_specs=[
            pl.BlockSpec((1, n_pages, T, slots), lambda b, j, pt: (b, 0, 0, 0)),
            pl.BlockSpec((1, T, slots), per_batch),
        ],
        scratch_shapes=[pltpu.VMEM((n_pages + 1, T, slots), jnp.int32)],
    )
    page_bias, bias_new = pl.pallas_call(
        functools.partial(_idx_sample_body, g=g, n_pages=n_pages, nt=T, topk=topk,
                          index_bits=max(1, ((n_pages + 1) * slots - 1).bit_length())),
        grid_spec=grid_spec,
        out_shape=[jax.ShapeDtypeStruct((B, n_pages, T, slots), jnp.float32),
                   jax.ShapeDtypeStruct((B, T, slots), jnp.float32)],
        compiler_params=pltpu.CompilerParams(
            dimension_semantics=("arbitrary", "arbitrary"), vmem_limit_bytes=48 * 1024 * 1024),
        name="idx_sample",
    )(page_table, qi, wcol, ki_new, *([cache_kit] * g))
    sel = bias_new[:, :, :T] == 0.0
    same = jnp.arange(H)[:, None] == jnp.arange(H)[None, :]
    nb = jnp.where(same[None, :, None, None, :] & sel[:, None, :, :, None], 0.0, MASKED_SCORE)
    new_bias = nb.reshape(B, H * T, T * H).astype(jnp.float32)
    row_bias = jnp.zeros((B, H * T, 1), jnp.float32)
    return paged_attention(q, k_new, v_new, _slots_minor(cache_k), _slots_minor(cache_v), page_table,
                           row_bias, page_bias, "token", new_bias)


def _fox_prompt_body(q_ref, k_ref, v_ref, ccol_ref, crow_ref, o_ref, cb, m_s, acc_s, *, tq):
    i = pl.program_id(1)
    row = lax.broadcasted_iota(jnp.int32, (tq, tq), 0)
    col = lax.broadcasted_iota(jnp.int32, (tq, tq), 1)
    ccol = ccol_ref[0]
    for h in range(H_FOX):
        cb[h] = jnp.broadcast_to(ccol[:, h:h + 1], (tq, tq))
    _init_softmax_state(m_s, acc_s)
    qh = q_ref[0]

    def attend_chunk(c, diagonal):
        start = pl.multiple_of(c * tq, tq)
        crow = crow_ref[0, :, pl.ds(start, tq)]
        kc = k_ref[0, :, pl.ds(start, tq), :]
        vc = v_ref[0, :, pl.ds(start, tq), :]
        s = jnp.einsum('hqd,hkd->hqk', qh, kc, preferred_element_type=jnp.float32)
        s = s + cb[...] - crow[:, None, :]
        if diagonal:
            s = jnp.where((col <= row)[None], s, MASKED_SCORE)
        _online_softmax_step(s, vc, m_s, acc_s)

    def full_chunk(c, carry):
        attend_chunk(c, False)
        return carry

    lax.fori_loop(0, i, full_chunk, 0)
    attend_chunk(i, True)
    _write_normalised(acc_s, o_ref)


def fox_prompt(q, k, v, logf):
    B, S = q.shape[:2]
    tq = Q_BLOCK
    bf = jnp.bfloat16
    c = jnp.cumsum(logf, axis=1)
    qh = _head_major(q * HEAD_DIM ** -0.5).astype(bf)
    kh = _head_major(k).astype(bf)
    vh = _with_ones(_head_major(v)).astype(bf)
    o = pl.pallas_call(
        functools.partial(_fox_prompt_body, tq=tq),
        grid=(B, S // tq),
        in_specs=[
            pl.BlockSpec((1, H_FOX, tq, HEAD_DIM), lambda b, i: (b, 0, i, 0)),
            pl.BlockSpec((1, H_FOX, S, HEAD_DIM), lambda b, i: (b, 0, 0, 0)),
            pl.BlockSpec((1, H_FOX, S, 2 * HEAD_DIM), lambda b, i: (b, 0, 0, 0)),
            pl.BlockSpec((1, tq, H_FOX), lambda b, i: (b, i, 0)),
            pl.BlockSpec((1, H_FOX, S), lambda b, i: (b, 0, 0)),
        ],
        out_specs=pl.BlockSpec((1, H_FOX, tq, HEAD_DIM), lambda b, i: (b, 0, i, 0)),
        out_shape=jax.ShapeDtypeStruct((B, H_FOX, S, HEAD_DIM), jnp.float32),
        scratch_shapes=[
            pltpu.VMEM((H_FOX, tq, tq), jnp.float32),
            pltpu.VMEM((H_FOX, tq, tq), jnp.float32),
            pltpu.VMEM((H_FOX, tq, 2 * HEAD_DIM), jnp.float32),
        ],
        compiler_params=pltpu.CompilerParams(
            dimension_semantics=("arbitrary", "arbitrary"), vmem_limit_bytes=48 * 1024 * 1024),
        name="fox_prompt",
    )(qh, kh, vh, c, c.transpose(0, 2, 1))
    return _head_major(o)


PAGES_PER_STEP = 16
IDX_PAGES_PER_STEP = 16


def _paged_attn_body(pt_ref, qbd_ref, q_ref, rb_ref, pb_ref, kn_ref, vn_ref, bn_ref, *rest, g, pb_axis, nh, nt):
    k_refs, v_refs = rest[:g], rest[g:2 * g]
    o_ref, m_s, l_s, acc_s = rest[2 * g:]
    j = pl.program_id(1)
    rows = nh * nt
    bf = jnp.bfloat16

    @pl.when(j == 0)
    def _():
        m_s[...] = jnp.full(m_s.shape, RUNNING_MAX_INIT, jnp.float32)
        l_s[...] = jnp.zeros(l_s.shape, jnp.float32)
        acc_s[...] = jnp.zeros(acc_s.shape, jnp.float32)

    qbd = qbd_ref[0]
    rb = rb_ref[0]
    logits = []
    for gi in range(g):
        kp = k_refs[gi][0]
        slots = kp.shape[2]
        k2 = kp.reshape(kp.shape[0] * kp.shape[1], slots).astype(bf)
        s = jnp.dot(qbd, k2, preferred_element_type=jnp.float32)
        pb = pb_ref[0, gi]
        if pb_axis == "head":
            pb = jnp.broadcast_to(pb[:, None, :], (nh, nt, slots)).reshape(rows, slots)
        else:
            pb = jnp.broadcast_to(pb[None], (nh, nt, slots)).reshape(rows, slots)
        logits.append(s + rb + pb)
    m_old = m_s[...]
    m_new = m_old
    for s in logits:
        m_new = jnp.maximum(m_new, jnp.max(s, axis=-1, keepdims=True))
    alpha = jnp.exp(m_old - m_new)
    l_new = alpha * l_s[...]
    acc_new = alpha * acc_s[...]
    for gi in range(g):
        vp = v_refs[gi][0]
        v2 = vp.reshape(vp.shape[0] * vp.shape[1], vp.shape[2]).astype(bf)
        p = jnp.exp(logits[gi] - m_new)
        l_new = l_new + jnp.sum(p, axis=-1, keepdims=True)
        acc_new = acc_new + lax.dot_general(p.astype(bf), v2, NT_DIMS, preferred_element_type=jnp.float32)
    m_s[...] = m_new
    l_s[...] = l_new
    acc_s[...] = acc_new

    @pl.when(j == pl.num_programs(1) - 1)
    def _():
        s = lax.dot_general(q_ref[0], kn_ref[0], NT_DIMS, preferred_element_type=jnp.float32) + bn_ref[0]
        m_prev = m_s[...]
        m_fin = jnp.maximum(m_prev, jnp.max(s, axis=-1, keepdims=True))
        scale_prev = jnp.exp(m_prev - m_fin)
        p = jnp.exp(s - m_fin)
        l_fin = scale_prev * l_s[...] + jnp.sum(p, axis=-1, keepdims=True)
        acc_fin = scale_prev * acc_s[...] + jnp.dot(p.astype(bf), vn_ref[0], preferred_element_type=jnp.float32)
        o_ref[0] = acc_fin / l_fin


def _slots_minor(cache):
    return jnp.moveaxis(cache, 1, -1)


def _block_diag_heads(x):
    B, R, H, D = x.shape
    eye = jnp.eye(H, dtype=x.dtype)
    return (x[:, :, :, None, :] * eye[None, None, :, :, None]).reshape(B, R, H, H * D)


def paged_attention(q, k_new, v_new, cache_k, cache_v, page_table, row_bias, page_bias, pb_axis, new_bias):
    B, T, H, D = q.shape
    n_pages = page_table.shape[1]
    slots = cache_k.shape[3]
    g = PAGES_PER_STEP
    pb_rows = page_bias.shape[2]
    bf = jnp.bfloat16
    qs = _head_major(q) * HEAD_DIM ** -0.5
    qr = qs.reshape(B, H * T, D).astype(bf)
    qbd = _block_diag_heads(qs.transpose(0, 2, 1, 3)).transpose(0, 2, 1, 3).reshape(B, H * T, H * D).astype(bf)
    kn = k_new.reshape(B, T * H, D).astype(bf)
    vn = _block_diag_heads(v_new).reshape(B, T * H, H * D).astype(bf)

    def page_map(gi):
        return lambda b, j, pt: (pt[b, j * g + gi], 0, 0, 0)

    per_batch = lambda b, j, pt: (b, 0, 0)
    page_spec = [pl.BlockSpec((1, H, D, slots), page_map(gi)) for gi in range(g)]
    grid_spec = pltpu.PrefetchScalarGridSpec(
        num_scalar_prefetch=1,
        grid=(B, n_pages // g),
        in_specs=[
            pl.BlockSpec((1, H * T, H * D), per_batch),
            pl.BlockSpec((1, H * T, D), per_batch),
            pl.BlockSpec((1, H * T, 1), per_batch),
            pl.BlockSpec((1, g, pb_rows, slots), lambda b, j, pt: (b, j, 0, 0)),
            pl.BlockSpec((1, T * H, D), per_batch),
            pl.BlockSpec((1, T * H, H * D), per_batch),
            pl.BlockSpec((1, H * T, T * H), per_batch),
        ] + page_spec + page_spec,
        out_specs=pl.BlockSpec((1, H * T, H * D), per_batch),
        scratch_shapes=[
            pltpu.VMEM((H * T, 1), jnp.float32),
            pltpu.VMEM((H * T, 1), jnp.float32),
            pltpu.VMEM((H * T, H * D), jnp.float32),
        ],
    )
    o = pl.pallas_call(
        functools.partial(_paged_attn_body, g=g, pb_axis=pb_axis, nh=H, nt=T),
        grid_spec=grid_spec,
        out_shape=jax.ShapeDtypeStruct((B, H * T, H * D), jnp.float32),
        compiler_params=pltpu.CompilerParams(
            dimension_semantics=("arbitrary", "arbitrary"), vmem_limit_bytes=48 * 1024 * 1024),
        name="paged_attention",
    )(page_table, qbd, qr, row_bias, page_bias, kn, vn, new_bias, *([cache_k] * g), *([cache_v] * g))
    o = o.reshape(B, H, T, H, D)
    hh = jnp.arange(H)
    return o[:, hh, :, hh, :].transpose(1, 2, 0, 3)


def fox_sample(q, k_new, v_new, logf_new, cache_k, cache_v, cache_logf, page_table):
    B, T = q.shape[:2]
    n_pages = page_table.shape[1]
    P = n_pages * PAGE_SIZE
    logf_past = cache_logf[page_table].reshape(B, P, H_FOX).astype(jnp.float32)
    r = jnp.cumsum(logf_past[:, ::-1], axis=1)[:, ::-1] - logf_past
    cn = jnp.cumsum(logf_new, axis=1)
    row_bias = cn.transpose(0, 2, 1).reshape(B, H_FOX * T, 1)
    page_bias = r.reshape(B, n_pages, PAGE_SIZE, H_FOX).transpose(0, 1, 3, 2)
    d = cn.transpose(0, 2, 1)[:, :, :, None] - cn.transpose(0, 2, 1)[:, :, None, :]
    tpos = jnp.arange(T)
    d = jnp.where(tpos[None, :] <= tpos[:, None], d, MASKED_SCORE)
    same = jnp.arange(H_FOX)[:, None] == jnp.arange(H_FOX)[None, :]
    nb = jnp.where(same[None, :, None, None, :], d[:, :, :, :, None], MASKED_SCORE)
    new_bias = nb.reshape(B, H_FOX * T, T * H_FOX)
    return paged_attention(q, k_new, v_new, _slots_minor(cache_k), _slots_minor(cache_v), page_table,
                           row_bias, page_bias, "head", new_bias)


def swiglu(x, wg, wu, wd):
    return (jax.nn.silu(x @ wg) * (x @ wu)) @ wd


TOKEN_TILE = 512


def _layer_norm_rows(z, g, b):
    mu = jnp.mean(z, axis=-1, keepdims=True)
    zc = z - mu
    var = jnp.mean(zc * zc, axis=-1, keepdims=True)
    return zc * lax.rsqrt(var + LN_EPS) * g + b


def _mix_body(heads_ref, x_ref, wo_ref, g_ref, b_ref, wr_ref, h_ref, s_ref):
    bf = jnp.bfloat16
    mix = jnp.dot(heads_ref[...].astype(bf), wo_ref[...], preferred_element_type=jnp.float32)
    h = _layer_norm_rows(DEEPNORM_ALPHA * x_ref[...] + mix, g_ref[...], b_ref[...])
    h_ref[...] = h
    logits = lax.dot_general(wr_ref[...], h.astype(bf), NT_DIMS, preferred_element_type=jnp.float32)
    s_ref[...] = 1.0 / (1.0 + jnp.exp(-logits))


def mix_ln_router(heads, x2, w_o, ln1_g, ln1_b, w_router):
    n, d = x2.shape
    tm = TOKEN_TILE
    bf = jnp.bfloat16
    row = lambda i: (i, 0)
    whole = lambda i: (0, 0)
    return pl.pallas_call(
        _mix_body,
        grid=(n // tm,),
        in_specs=[pl.BlockSpec((tm, d), row), pl.BlockSpec((tm, d), row),
                  pl.BlockSpec((d, d), whole), pl.BlockSpec((1, d), whole), pl.BlockSpec((1, d), whole),
                  pl.BlockSpec((N_EXPERTS, d), whole)],
        out_specs=[pl.BlockSpec((tm, d), row), pl.BlockSpec((N_EXPERTS, tm), lambda i: (0, i))],
        out_shape=[jax.ShapeDtypeStruct((n, d), jnp.float32),
                   jax.ShapeDtypeStruct((N_EXPERTS, n), jnp.float32)],
        compiler_params=pltpu.CompilerParams(
            dimension_semantics=("arbitrary",), vmem_limit_bytes=48 * 1024 * 1024),
        name="mix_ln_router",
    )(heads, x2, w_o.astype(bf), ln1_g[None], ln1_b[None], w_router.T.astype(bf))


def _final_body(h_ref, r_ref, wg_ref, wu_ref, wd_ref, g_ref, b_ref, y_ref):
    bf = jnp.bfloat16
    h = h_ref[...]
    hb = h.astype(bf)
    a = jnp.dot(hb, wg_ref[...], preferred_element_type=jnp.float32)
    u = jnp.dot(hb, wu_ref[...], preferred_element_type=jnp.float32)
    shared = jnp.dot(((a / (1.0 + jnp.exp(-a))) * u).astype(bf), wd_ref[...], preferred_element_type=jnp.float32)
    y_ref[...] = _layer_norm_rows(DEEPNORM_ALPHA * h + (r_ref[...] + shared), g_ref[...], b_ref[...])


def shared_ln_final(h, routed, w_gate_s, w_up_s, w_down_s, ln2_g, ln2_b):
    n, d = h.shape
    tm = TOKEN_TILE
    ds = w_gate_s.shape[1]
    bf = jnp.bfloat16
    row = lambda i: (i, 0)
    whole = lambda i: (0, 0)
    return pl.pallas_call(
        _final_body,
        grid=(n // tm,),
        in_specs=[pl.BlockSpec((tm, d), row), pl.BlockSpec((tm, d), row),
                  pl.BlockSpec((d, ds), whole), pl.BlockSpec((d, ds), whole), pl.BlockSpec((ds, d), whole),
                  pl.BlockSpec((1, d), whole), pl.BlockSpec((1, d), whole)],
        out_specs=pl.BlockSpec((tm, d), row),
        out_shape=jax.ShapeDtypeStruct((n, d), jnp.float32),
        compiler_params=pltpu.CompilerParams(
            dimension_semantics=("arbitrary",), vmem_limit_bytes=48 * 1024 * 1024),
        name="shared_ln_final",
    )(h, routed, w_gate_s.astype(bf), w_up_s.astype(bf), w_down_s.astype(bf), ln2_g[None], ln2_b[None])


def _take_first_max(v, rowid):
    m = jnp.max(v, axis=0, keepdims=True)
    first = jnp.min(jnp.where(v == m, rowid, float(v.shape[0])), axis=0, keepdims=True)
    return m, first, jnp.where(rowid == first, -jnp.inf, v)


def _route_body(s_ref, b_ref, idx_ref, gate_ref):
    scores = s_ref[...]
    ne, tm = scores.shape
    gsz = ne // N_GROUPS
    biased = scores + b_ref[...]
    rid_g = lax.broadcasted_iota(jnp.int32, (gsz, tm), 0).astype(jnp.float32)
    gscore = []
    for g in range(N_GROUPS):
        m1, _, rest = _take_first_max(biased[g * gsz:(g + 1) * gsz], rid_g)
        gscore.append(m1 + jnp.max(rest, axis=0, keepdims=True))
    gs = jnp.concatenate(gscore, axis=0)
    gid = lax.broadcasted_iota(jnp.int32, (N_GROUPS, tm), 0).astype(jnp.float32)
    keep = jnp.zeros((N_GROUPS, tm), jnp.float32)
    for _ in range(TOPK_GROUPS):
        _, first, gs_next = _take_first_max(gs, gid)
        keep = jnp.where(gid == first, 1.0, keep)
        gs = gs_next
    masked = jnp.concatenate(
        [jnp.where(keep[g:g + 1] > 0.0, biased[g * gsz:(g + 1) * gsz], -jnp.inf) for g in range(N_GROUPS)], axis=0)
    rid = lax.broadcasted_iota(jnp.int32, (ne, tm), 0).astype(jnp.float32)
    picks, weights = [], []
    for _ in range(TOP_K):
        _, first, masked = _take_first_max(masked, rid)
        picks.append(first)
        weights.append(jnp.sum(jnp.where(rid == first, scores, 0.0), axis=0, keepdims=True))
    w = jnp.concatenate(weights, axis=0)
    idx_ref[...] = jnp.concatenate(picks, axis=0).astype(jnp.int32)
    gate_ref[...] = w / jnp.sum(w, axis=0, keepdims=True) * ROUTED_SCALE


def route(scores_t, router_bias):
    ne, n = scores_t.shape
    tm = TOKEN_TILE
    idx_t, gate_t = pl.pallas_call(
        _route_body,
        grid=(n // tm,),
        in_specs=[pl.BlockSpec((ne, tm), lambda i: (0, i)), pl.BlockSpec((ne, 1), lambda i: (0, 0))],
        out_specs=[pl.BlockSpec((TOP_K, tm), lambda i: (0, i)), pl.BlockSpec((TOP_K, tm), lambda i: (0, i))],
        out_shape=[jax.ShapeDtypeStruct((TOP_K, n), jnp.int32), jax.ShapeDtypeStruct((TOP_K, n), jnp.float32)],
        compiler_params=pltpu.CompilerParams(dimension_semantics=("arbitrary",)),
        name="route_topk",
    )(scores_t, router_bias.astype(jnp.float32)[:, None])
    return idx_t.T, gate_t.T


RANK_TILE = 256


def _rank_body(idx_ref, rank_ref, count_ref, running):
    i = pl.program_id(0)
    tm, nk = idx_ref.shape

    @pl.when(i == 0)
    def _():
        running[...] = jnp.zeros(running.shape, jnp.float32)

    idx = idx_ref[...]
    lane = lax.broadcasted_iota(jnp.int32, (tm, N_EXPERTS), 1)
    hots = [jnp.where(lane == idx[:, k:k + 1], 1.0, 0.0) for k in range(nk)]
    onehot = hots[0]
    for k in range(1, nk):
        onehot = onehot + hots[k]
    r = lax.broadcasted_iota(jnp.int32, (tm, tm), 0)
    c = lax.broadcasted_iota(jnp.int32, (tm, tm), 1)
    earlier = jnp.where(c < r, 1.0, 0.0).astype(jnp.bfloat16)
    before = jnp.dot(earlier, onehot.astype(jnp.bfloat16), preferred_element_type=jnp.float32) + running[...]
    lane_k = lax.broadcasted_iota(jnp.int32, (tm, nk), 1)
    out = jnp.zeros((tm, nk), jnp.float32)
    for k in range(nk):
        out = jnp.where(lane_k == k, jnp.sum(hots[k] * before, axis=-1, keepdims=True), out)
    rank_ref[...] = out.astype(jnp.int32)
    running[...] = running[...] + jnp.sum(onehot, axis=0, keepdims=True)
    count_ref[...] = running[...]


def expert_rank(idx):
    n, nk = idx.shape
    tm = RANK_TILE
    rank, counts = pl.pallas_call(
        _rank_body,
        grid=(n // tm,),
        in_specs=[pl.BlockSpec((tm, nk), lambda i: (i, 0))],
        out_specs=[pl.BlockSpec((tm, nk), lambda i: (i, 0)),
                   pl.BlockSpec((1, N_EXPERTS), lambda i: (0, 0))],
        out_shape=[jax.ShapeDtypeStruct((n, nk), jnp.int32),
                   jax.ShapeDtypeStruct((1, N_EXPERTS), jnp.float32)],
        scratch_shapes=[pltpu.VMEM((1, N_EXPERTS), jnp.float32)],
        compiler_params=pltpu.CompilerParams(dimension_semantics=("arbitrary",)),
        name="expert_rank",
    )(idx)
    return rank, counts[0].astype(jnp.int32)


EXPERT_TILE = 256


def _expert_body(be_ref, nu_ref, x_ref, wg_ref, wu_ref, wd_ref, y_ref):
    i = pl.program_id(0)

    @pl.when(i < nu_ref[0])
    def _():
        bf = jnp.bfloat16
        x = x_ref[...]
        a = jnp.dot(x, wg_ref[0].astype(bf), preferred_element_type=jnp.float32)
        u = jnp.dot(x, wu_ref[0].astype(bf), preferred_element_type=jnp.float32)
        hmid = (a / (1.0 + jnp.exp(-a))) * u
        y_ref[...] = jnp.dot(hmid.astype(bf), wd_ref[0].astype(bf), preferred_element_type=jnp.float32)

    @pl.when(i >= nu_ref[0])
    def _():
        y_ref[...] = jnp.zeros(y_ref.shape, jnp.float32)


def expert_ffn(xb, tile_expert, n_used, w_gate_e, w_up_e, w_down_e):
    rows, d = xb.shape
    tb = EXPERT_TILE
    de = w_gate_e.shape[2]
    grid_spec = pltpu.PrefetchScalarGridSpec(
        num_scalar_prefetch=2,
        grid=(rows // tb,),
        in_specs=[
            pl.BlockSpec((tb, d), lambda i, te, nu: (i, 0)),
            pl.BlockSpec((1, d, de), lambda i, te, nu: (te[i], 0, 0)),
            pl.BlockSpec((1, d, de), lambda i, te, nu: (te[i], 0, 0)),
            pl.BlockSpec((1, de, d), lambda i, te, nu: (te[i], 0, 0)),
        ],
        out_specs=pl.BlockSpec((tb, d), lambda i, te, nu: (i, 0)),
    )
    return pl.pallas_call(
        _expert_body,
        grid_spec=grid_spec,
        out_shape=jax.ShapeDtypeStruct((rows, d), jnp.float32),
        compiler_params=pltpu.CompilerParams(
            dimension_semantics=("arbitrary",), vmem_limit_bytes=48 * 1024 * 1024),
        name="expert_ffn",
    )(tile_expert, n_used, xb, w_gate_e, w_up_e, w_down_e)


def routed_experts(x2, idx, gate, w_gate_e, w_up_e, w_down_e):
    n, d = x2.shape
    a = n * TOP_K
    n_tiles = (a + N_EXPERTS * (EXPERT_TILE - 1) + EXPERT_TILE - 1) // EXPERT_TILE
    rows = n_tiles * EXPERT_TILE
    rank, counts = expert_rank(idx.astype(jnp.int32))
    padded = (counts + EXPERT_TILE - 1) // EXPERT_TILE * EXPERT_TILE
    pad_end = jnp.cumsum(padded)
    pad_start = pad_end - padded
    dest2 = pad_start[idx] + rank
    dest = dest2.reshape(-1)
    token = jnp.repeat(jnp.arange(n, dtype=jnp.int32), TOP_K)
    row_token = jnp.full((rows,), n, jnp.int32).at[dest].set(token)
    tile_expert = jnp.minimum(
        jnp.searchsorted(pad_end, jnp.arange(n_tiles) * EXPERT_TILE, side='right'), N_EXPERTS - 1)
    n_used = (pad_end[-1:] // EXPERT_TILE).astype(jnp.int32)
    x_pad = jnp.concatenate([x2, jnp.zeros((1, d), x2.dtype)], axis=0).astype(jnp.bfloat16)
    xb = x_pad[row_token]
    yb = expert_ffn(xb, tile_expert.astype(jnp.int32), n_used, w_gate_e, w_up_e, w_down_e)
    return (yb[dest2] * gate[:, :, None]).sum(axis=1)


def concat_heads(o_a, o_f):
    B, T = o_a.shape[:2]
    return jnp.concatenate([o_a.reshape(B * T, W_DSA), o_f.reshape(B * T, W_FOX)], axis=-1)


def finish_tokens(x2, heads, w_o, ln1_g, ln1_b, w_router, router_bias, w_gate_e, w_up_e, w_down_e,
                  w_gate_s, w_up_s, w_down_s, ln2_g, ln2_b):
    h, scores = mix_ln_router(heads, x2, w_o, ln1_g, ln1_b, w_router)
    idx, gate = route(scores, router_bias)
    routed = routed_experts(h, idx, gate, w_gate_e, w_up_e, w_down_e)
    return shared_ln_final(h, routed, w_gate_s, w_up_s, w_down_s, ln2_g, ln2_b)


def kernel(x_prompt, x_sample, cache_k_dsa, cache_v_dsa, cache_k_idx, cache_k_fox, cache_v_fox,
           cache_logf_fox, page_table, w_in, b_forget, idx_ln_g, idx_ln_b, w_o, ln1_g, ln1_b,
           w_router, router_bias, w_gate_e, w_up_e, w_down_e, w_gate_s, w_up_s, w_down_s,
           ln2_g, ln2_b):
    seq = x_prompt.shape[1]
    dec_seq = x_sample.shape[1]
    past = page_table.shape[1] * PAGE_SIZE
    pos_p = jnp.arange(seq, dtype=jnp.int32)
    pos_s = past + jnp.arange(dec_seq, dtype=jnp.int32)
    xp, xs = x_prompt, x_sample
    l = 0
    q_a, k_a, v_a, q_i, k_i, w_i, q_f, k_f, v_f, logf = project_mixers(
        xp, pos_p, w_in[l], b_forget[l], idx_ln_g[l], idx_ln_b[l])
    o_a = dsa_prompt(q_a, k_a, v_a, q_i, k_i, w_i)
    o_f = fox_prompt(q_f, k_f, v_f, logf)
    heads_p = concat_heads(o_a, o_f)
    outs_p = (k_a, v_a, k_i, k_f, v_f, logf.astype(x_prompt.dtype))
    q_a, k_a, v_a, q_i, k_i, w_i, q_f, k_f, v_f, logf = project_mixers(
        xs, pos_s, w_in[l], b_forget[l], idx_ln_g[l], idx_ln_b[l])
    o_a = dsa_sample(q_a, k_a, v_a, q_i, k_i, w_i, cache_k_dsa[l], cache_v_dsa[l], cache_k_idx[l], page_table)
    o_f = fox_sample(q_f, k_f, v_f, logf, cache_k_fox[l], cache_v_fox[l], cache_logf_fox[l], page_table)
    heads_s = concat_heads(o_a, o_f)
    outs_s = (k_a, v_a, k_i, k_f, v_f, logf.astype(x_sample.dtype))
    n_p = heads_p.shape[0]
    x2 = jnp.concatenate([xp.reshape(n_p, D_MODEL), xs.reshape(-1, D_MODEL)], axis=0)
    y = finish_tokens(x2, jnp.concatenate([heads_p, heads_s], axis=0), w_o[l], ln1_g[l], ln1_b[l],
                      w_router[l], router_bias[l], w_gate_e[l], w_up_e[l], w_down_e[l],
                      w_gate_s[l], w_up_s[l], w_down_s[l], ln2_g[l], ln2_b[l])
    yp = y[:n_p].reshape(x_prompt.shape)
    ys = y[n_p:].reshape(x_sample.shape)
    return (yp, ys) + tuple(o[None] for o in outs_p) + tuple(o[None] for o in outs_s)
```

```python
import functools
import math
import jax
import jax.numpy as jnp
from jax import lax
import numpy as np
from jax.experimental import pallas as pl
from jax.experimental.pallas import tpu as pltpu

D_MODEL = 1024
PAGE_SIZE = 128
HEAD_DIM = 64
N_HEADS = D_MODEL // HEAD_DIM
H_DSA = N_HEADS // 2
H_FOX = N_HEADS - H_DSA
W_DSA = H_DSA * HEAD_DIM
W_FOX = H_FOX * HEAD_DIM
MIX_WIDTH = W_DSA + W_FOX
H_IDX = 8
D_IDX = 64
TOPK_MAX = 256
Q_BLOCK = 128
ROPE_THETA = 10000.0
N_EXPERTS = 256
TOP_K = 8
N_GROUPS = 8
TOPK_GROUPS = 4
D_EXPERT = 256
D_SHARED = 256
ROUTED_SCALE = 2.5
EXPERT_BLOCK = 128
LN_EPS = 1e-5
DEPTH = 1
DEEPNORM_ALPHA = (2 * DEPTH) ** 0.25

_IN_SIZES = (W_DSA, W_DSA, W_DSA, H_IDX * D_IDX, D_IDX, H_IDX, W_FOX, W_FOX, W_FOX, H_FOX)
IN_COLS = sum(_IN_SIZES)
SPLIT_POINTS = tuple(int(v) for v in np.cumsum(_IN_SIZES)[:-1])

LANES = 128


def _proj_body(x_ref, w_ref, o_ref):
    o_ref[...] = jnp.dot(x_ref[...].astype(jnp.bfloat16), w_ref[...],
                         preferred_element_type=jnp.float32)


def _project(x2, w):
    m, d = x2.shape
    c = w.shape[1]
    cp = (c + LANES - 1) // LANES * LANES
    wp = jnp.pad(w, ((0, 0), (0, cp - c))).astype(jnp.bfloat16)
    tm = 256
    out = pl.pallas_call(
        _proj_body,
        grid=(m // tm,),
        in_specs=[pl.BlockSpec((tm, d), lambda i: (i, 0)),
                  pl.BlockSpec((d, cp), lambda i: (0, 0))],
        out_specs=pl.BlockSpec((tm, cp), lambda i: (i, 0)),
        out_shape=jax.ShapeDtypeStruct((m, cp), jnp.float32),
        compiler_params=pltpu.CompilerParams(vmem_limit_bytes=48 * 1024 * 1024),
        name="in_proj",
    )(x2, wp)
    return out[:, :c]


def layer_norm(x, g, b):
    xf = x.astype(jnp.float32)
    mu = xf.mean(-1, keepdims=True)
    var = jnp.square(xf - mu).mean(-1, keepdims=True)
    return ((xf - mu) * lax.rsqrt(var + LN_EPS) * g.astype(jnp.float32) + b.astype(jnp.float32)).astype(x.dtype)


def rope(x, pos):
    half = x.shape[-1] // 2
    freqs = ROPE_THETA ** (-jnp.arange(half, dtype=jnp.float32) / half)
    ang = pos.astype(jnp.float32)[:, None] * freqs[None, :]
    cos = jnp.cos(ang)[None, :, None, :]
    sin = jnp.sin(ang)[None, :, None, :]
    xf = x.astype(jnp.float32)
    x1, x2 = xf[..., :half], xf[..., half:]
    return jnp.concatenate([x1 * cos - x2 * sin, x2 * cos + x1 * sin], axis=-1).astype(x.dtype)


def project_mixers(x, pos, w_in, b_forget, idx_ln_g, idx_ln_b):
    B, T, _ = x.shape
    h = _project(x.reshape(B * T, D_MODEL), w_in).reshape(B, T, IN_COLS)
    q_a, k_a, v_a, q_i, k_i, w_i, q_f, k_f, v_f, f_logit = jnp.split(h, SPLIT_POINTS, axis=-1)
    q_a = rope(q_a.reshape(B, T, H_DSA, HEAD_DIM), pos)
    k_a = rope(k_a.reshape(B, T, H_DSA, HEAD_DIM), pos)
    v_a = v_a.reshape(B, T, H_DSA, HEAD_DIM)
    q_i = rope(q_i.reshape(B, T, H_IDX, D_IDX), pos)
    k_i = rope(layer_norm(k_i, idx_ln_g, idx_ln_b)[:, :, None, :], pos)[:, :, 0, :]
    w_i = w_i.astype(jnp.float32) * (H_IDX ** -0.5 * D_IDX ** -0.5)
    q_f = q_f.reshape(B, T, H_FOX, HEAD_DIM)
    k_f = k_f.reshape(B, T, H_FOX, HEAD_DIM)
    v_f = v_f.reshape(B, T, H_FOX, HEAD_DIM)
    logf = jax.nn.log_sigmoid((f_logit + b_forget).astype(jnp.float32))
    return q_a, k_a, v_a, q_i, k_i, w_i, q_f, k_f, v_f, logf


def indexer_scores(q_i, k_i, w_i):
    s = jnp.einsum('bqhd,bld->bqhl', q_i, k_i).astype(jnp.float32)
    return jnp.einsum('bqhl,bqh->bql', jax.nn.relu(s), w_i)


def select_topk(scores, mask, k):
    masked = jnp.where(mask, scores, -jnp.inf)
    vals, idx = lax.top_k(masked, k)
    return idx, vals > -jnp.inf


def gather_rows(src, idx):
    return jax.vmap(lambda s, i: s[i])(src, idx)


def sparse_attend(q, kg, vg, valid):
    s = jnp.einsum('bqhd,bqkhd->bqhk', q, kg).astype(jnp.float32) * HEAD_DIM ** -0.5
    s = jnp.where(valid[:, :, None, :], s, -jnp.inf)
    p = jax.nn.softmax(s, axis=-1)
    return jnp.einsum('bqhk,bqkhd->bqhd', p, vg)


INT_MIN = -2 ** 31
MASKED_SCORE = -1e30
RUNNING_MAX_INIT = -1e29
NT_DIMS = (((1,), (1,)), ((), ()))


def _head_major(x):
    return x.transpose(0, 2, 1, 3)


def _with_ones(v):
    return jnp.concatenate([v, jnp.ones_like(v)], axis=-1)


def _online_softmax_step(s, vc, m_ref, acc_ref):
    m_old = m_ref[...]
    m_new = jnp.maximum(m_old, jnp.max(s, axis=-1, keepdims=True))
    alpha = jnp.exp(m_old - m_new)
    p = jnp.exp(s - m_new)
    pv = jnp.einsum('hqk,hkd->hqd', p.astype(jnp.bfloat16), vc, preferred_element_type=jnp.float32)
    acc_ref[...] = alpha * acc_ref[...] + pv
    m_ref[...] = m_new


def _init_softmax_state(m_ref, acc_ref):
    m_ref[...] = jnp.full(m_ref.shape, RUNNING_MAX_INIT, jnp.float32)
    acc_ref[...] = jnp.zeros(acc_ref.shape, jnp.float32)


def _write_normalised(acc_ref, o_ref):
    a = acc_ref[...]
    o_ref[0] = a[:, :, :HEAD_DIM] / a[:, :, HEAD_DIM:]


def _kth_largest_key(count_ge, rows, kf, total):
    c0 = count_ge(jnp.zeros((rows, 1), jnp.int32))
    nonneg = c0 >= kf
    prefix = jnp.where(nonneg, 0, INT_MIN).astype(jnp.int32)
    cnt = jnp.where(nonneg, c0, total)

    def bit_step(t, carry):
        prefix, cnt = carry
        cand = prefix + lax.shift_left(jnp.int32(1), 30 - t)
        cc = count_ge(cand)
        take = cc >= kf
        return jnp.where(take, cand, prefix), jnp.where(take, cc, cnt)

    thr, cnt = lax.fori_loop(0, 31, bit_step, (prefix, cnt))
    thr = jnp.maximum(thr, INT_MIN + 1)
    excess = cnt > kf
    any_excess = jnp.max(jnp.where(excess, 1.0, 0.0)) > 0.0
    return thr, excess, any_excess


def _last_kept_tie(count_ties_before, need, rows, index_bits):
    def idx_step(t, last):
        cand = last + lax.shift_left(jnp.int32(1), index_bits - 1 - t)
        return jnp.where(count_ties_before(cand) < need, cand, last)

    return lax.fori_loop(0, index_bits, idx_step, jnp.zeros((rows, 1), jnp.int32))


def _dsa_prompt_body(qi_ref, ki_ref, wi_ref, q_ref, k_ref, v_ref, o_ref,
                     keys, wb, m_s, acc_s, *, topk, tq, seq_bits):
    i = pl.program_id(1)
    n_chunks = i + 1
    row = lax.broadcasted_iota(jnp.int32, (tq, tq), 0)
    col = lax.broadcasted_iota(jnp.int32, (tq, tq), 1)
    kf = float(topk)

    w = wi_ref[0]
    for h in range(H_IDX):
        wb[h] = jnp.broadcast_to(w[:, h:h + 1], (tq, tq))
    qi_all = qi_ref[0].reshape(H_IDX * tq, D_IDX)

    def index_chunk(c, carry):
        kic = ki_ref[0, pl.ds(pl.multiple_of(c * tq, tq), tq), :]
        s = lax.dot_general(qi_all, kic, NT_DIMS, preferred_element_type=jnp.float32)
        sc = jnp.maximum(s[0:tq], 0.0) * wb[0]
        for h in range(1, H_IDX):
            sc = sc + jnp.maximum(s[h * tq:(h + 1) * tq], 0.0) * wb[h]
        sc = jnp.where(sc == 0.0, 0.0, sc)
        bits = pltpu.bitcast(sc, jnp.int32)
        keys[c] = jnp.where(bits < 0, bits ^ 0x7FFFFFFF, bits)
        return carry

    lax.fori_loop(0, n_chunks, index_chunk, 0)
    keys[i] = jnp.where(col <= row, keys[i], INT_MIN)

    def count_ge(cand):
        cb = jnp.broadcast_to(cand, (tq, tq))

        def body(c, acc):
            return acc + jnp.where(keys[c] >= cb, 1.0, 0.0)

        acc = lax.fori_loop(0, n_chunks, body, jnp.zeros((tq, tq), jnp.float32))
        return jnp.sum(acc, axis=-1, keepdims=True)

    thr, excess, any_excess = _kth_largest_key(count_ge, tq, kf, (n_chunks * tq).astype(jnp.float32))

    @pl.when(any_excess)
    def _():
        need = kf - count_ge(thr + 1)
        thb = jnp.broadcast_to(thr, (tq, tq))

        def count_ties_before(j):
            jb = jnp.broadcast_to(j, (tq, tq))

            def body(c, acc):
                hit = (keys[c] == thb) & ((col + c * tq) < jb)
                return acc + jnp.where(hit, 1.0, 0.0)

            acc = lax.fori_loop(0, n_chunks, body, jnp.zeros((tq, tq), jnp.float32))
            return jnp.sum(acc, axis=-1, keepdims=True)

        last = _last_kept_tie(count_ties_before, need, tq, seq_bits)
        lb = jnp.broadcast_to(last, (tq, tq))
        eb = jnp.broadcast_to(excess, (tq, tq))

        def drop(c, carry):
            kk = keys[c]
            keys[c] = jnp.where(eb & (kk == thb) & ((col + c * tq) > lb), INT_MIN, kk)
            return carry

        lax.fori_loop(0, n_chunks, drop, 0)

    thb = jnp.broadcast_to(thr, (tq, tq))
    _init_softmax_state(m_s, acc_s)
    qh = q_ref[0]

    def attend_chunk(c, carry):
        sel = keys[c] >= thb
        start = pl.multiple_of(c * tq, tq)
        kc = k_ref[0, :, pl.ds(start, tq), :]
        vc = v_ref[0, :, pl.ds(start, tq), :]
        s = jnp.einsum('hqd,hkd->hqk', qh, kc, preferred_element_type=jnp.float32)
        s = jnp.where(sel[None], s, MASKED_SCORE)
        _online_softmax_step(s, vc, m_s, acc_s)
        return carry

    lax.fori_loop(0, n_chunks, attend_chunk, 0)
    _write_normalised(acc_s, o_ref)


def dsa_prompt(q, k, v, q_i, k_i, w_i):
    B, S = q.shape[:2]
    topk = min(TOPK_MAX, S // 4)
    tq = Q_BLOCK
    nq = S // tq
    bf = jnp.bfloat16
    qh = _head_major(q * HEAD_DIM ** -0.5).astype(bf)
    kh = _head_major(k).astype(bf)
    vh = _with_ones(_head_major(v)).astype(bf)
    qih = _head_major(q_i).astype(bf)
    body = functools.partial(_dsa_prompt_body, topk=topk, tq=tq, seq_bits=max(1, (S - 1).bit_length()))
    o = pl.pallas_call(
        body,
        grid=(B, nq),
        in_specs=[
            pl.BlockSpec((1, H_IDX, tq, D_IDX), lambda b, i: (b, 0, i, 0)),
            pl.BlockSpec((1, S, D_IDX), lambda b, i: (b, 0, 0)),
            pl.BlockSpec((1, tq, H_IDX), lambda b, i: (b, i, 0)),
            pl.BlockSpec((1, H_DSA, tq, HEAD_DIM), lambda b, i: (b, 0, i, 0)),
            pl.BlockSpec((1, H_DSA, S, HEAD_DIM), lambda b, i: (b, 0, 0, 0)),
            pl.BlockSpec((1, H_DSA, S, 2 * HEAD_DIM), lambda b, i: (b, 0, 0, 0)),
        ],
        out_specs=pl.BlockSpec((1, H_DSA, tq, HEAD_DIM), lambda b, i: (b, 0, i, 0)),
        out_shape=jax.ShapeDtypeStruct((B, H_DSA, S, HEAD_DIM), jnp.float32),
        scratch_shapes=[
            pltpu.VMEM((nq, tq, tq), jnp.int32),
            pltpu.VMEM((H_IDX, tq, tq), jnp.float32),
            pltpu.VMEM((H_DSA, tq, tq), jnp.float32),
            pltpu.VMEM((H_DSA, tq, 2 * HEAD_DIM), jnp.float32),
        ],
        compiler_params=pltpu.CompilerParams(
            dimension_semantics=("arbitrary", "arbitrary"), vmem_limit_bytes=48 * 1024 * 1024),
        name="dsa_prompt",
    )(qih, k_i.astype(bf), w_i, qh, kh, vh)
    return _head_major(o)


def _score_keys(s, wcol, nt):
    sc = jnp.maximum(s[0:nt], 0.0) * wcol[0:nt]
    for h in range(1, H_IDX):
        sc = sc + jnp.maximum(s[h * nt:(h + 1) * nt], 0.0) * wcol[h * nt:(h + 1) * nt]
    sc = jnp.where(sc == 0.0, 0.0, sc)
    bits = pltpu.bitcast(sc, jnp.int32)
    return jnp.where(bits < 0, bits ^ 0x7FFFFFFF, bits)


def _idx_sample_body(pt_ref, qi_ref, wc_ref, kin_ref, *rest, g, n_pages, nt, topk, index_bits):
    ki_refs = rest[:g]
    pb_ref, sn_ref, keys = rest[g:]
    j = pl.program_id(1)
    bf = jnp.bfloat16
    slots = keys.shape[2]
    qi = qi_ref[0]
    wcol = wc_ref[0]
    for gi in range(g):
        s = jnp.dot(qi, ki_refs[gi][0].astype(bf), preferred_element_type=jnp.float32)
        keys[j * g + gi] = _score_keys(s, wcol, nt)

    @pl.when(j == pl.num_programs(1) - 1)
    def _():
        row = lax.broadcasted_iota(jnp.int32, (nt, slots), 0)
        col = lax.broadcasted_iota(jnp.int32, (nt, slots), 1)
        s = lax.dot_general(qi, kin_ref[0], NT_DIMS, preferred_element_type=jnp.float32)
        keys[n_pages] = jnp.where(col <= row, _score_keys(s, wcol, nt), INT_MIN)
        kf = float(topk)

        def count_ge(cand):
            hit = jnp.where(keys[...] >= cand[None], 1.0, 0.0)
            return jnp.sum(jnp.sum(hit, axis=0), axis=-1, keepdims=True)

        total = jnp.float32((n_pages + 1) * slots)
        thr, excess, any_excess = _kth_largest_key(count_ge, nt, kf, total)

        @pl.when(any_excess)
        def _():
            need = kf - count_ge(thr + 1)
            pos = (lax.broadcasted_iota(jnp.int32, keys.shape, 0) * slots
                   + lax.broadcasted_iota(jnp.int32, keys.shape, 2))

            def count_ties_before(jj):
                hit = jnp.where((keys[...] == thr[None]) & (pos < jj[None]), 1.0, 0.0)
                return jnp.sum(jnp.sum(hit, axis=0), axis=-1, keepdims=True)

            last = _last_kept_tie(count_ties_before, need, nt, index_bits)
            kk = keys[...]
            keys[...] = jnp.where(excess[None] & (kk == thr[None]) & (pos > last[None]), INT_MIN, kk)

        bias = jnp.where(keys[...] >= thr[None], 0.0, MASKED_SCORE)
        sn_ref[0] = bias[n_pages]
        pb_ref[0] = bias[:n_pages]


def dsa_sample(q, k_new, v_new, q_i, k_i_new, w_i, cache_k, cache_v, cache_ki, page_table):
    B, T, H, D = q.shape
    n_pages = page_table.shape[1]
    slots = cache_ki.shape[1]
    L = n_pages * slots + T
    topk = min(TOPK_MAX, L // 4)
    g = IDX_PAGES_PER_STEP
    bf = jnp.bfloat16
    qi = _head_major(q_i).reshape(B, H_IDX * T, D_IDX).astype(bf)
    wcol = w_i.transpose(0, 2, 1).reshape(B, H_IDX * T, 1)
    ki_new = jnp.pad(k_i_new, ((0, 0), (0, slots - T), (0, 0))).astype(bf)
    cache_kit = _slots_minor(cache_ki)

    def page_map(gi):
        return lambda b, j, pt: (pt[b, j * g + gi], 0, 0)

    per_batch = lambda b, j, pt: (b, 0, 0)
    grid_spec = pltpu.PrefetchScalarGridSpec(
        num_scalar_prefetch=1,
        grid=(B, n_pages // g),
        in_specs=[
            pl.BlockSpec((1, H_IDX * T, D_IDX), per_batch),
            pl.BlockSpec((1, H_IDX * T, 1), per_batch),
            pl.BlockSpec((1, slots, D_IDX), per_batch),
        ] + [pl.BlockSpec((1, D_IDX, slots), page_map(gi)) for gi in range(g)],
        out_specs=[
            pl.BlockSpec((1, n_pages, T, slots), lambda b, j, pt: (b, 0, 0, 0)),
            pl.BlockSpec((1, T, slots), per_batch),
        ],
        scratch_shapes=[pltpu.VMEM((n_pages + 1, T, slots), jnp.int32)],
    )
    page_bias, bias_new = pl.pallas_call(
        functools.partial(_idx_sample_body, g=g, n_pages=n_pages, nt=T, topk=topk,
                          index_bits=max(1, ((n_pages + 1) * slots - 1).bit_length())),
        grid_spec=grid_spec,
        out_shape=[jax.ShapeDtypeStruct((B, n_pages, T, slots), jnp.float32),
                   jax.ShapeDtypeStruct((B, T, slots), jnp.float32)],
        compiler_params=pltpu.CompilerParams(
            dimension_semantics=("arbitrary", "arbitrary"), vmem_limit_bytes=48 * 1024 * 1024),
        name="idx_sample",
    )(page_table, qi, wcol, ki_new, *([cache_kit] * g))
    sel = bias_new[:, :, :T] == 0.0
    same = jnp.arange(H)[:, None] == jnp.arange(H)[None, :]
    nb = jnp.where(same[None, :, None, None, :] & sel[:, None, :, :, None], 0.0, MASKED_SCORE)
    new_bias = nb.reshape(B, H * T, T * H).astype(jnp.float32)
    row_bias = jnp.zeros((B, H * T, 1), jnp.float32)
    return paged_attention(q, k_new, v_new, _slots_minor(cache_k), _slots_minor(cache_v), page_table,
                           row_bias, page_bias, "token", new_bias)


def _fox_prompt_body(q_ref, k_ref, v_ref, ccol_ref, crow_ref, o_ref, cb, m_s, acc_s, *, tq):
    i = pl.program_id(1)
    row = lax.broadcasted_iota(jnp.int32, (tq, tq), 0)
    col = lax.broadcasted_iota(jnp.int32, (tq, tq), 1)
    ccol = ccol_ref[0]
    for h in range(H_FOX):
        cb[h] = jnp.broadcast_to(ccol[:, h:h + 1], (tq, tq))
    _init_softmax_state(m_s, acc_s)
    qh = q_ref[0]

    def attend_chunk(c, diagonal):
        start = pl.multiple_of(c * tq, tq)
        crow = crow_ref[0, :, pl.ds(start, tq)]
        kc = k_ref[0, :, pl.ds(start, tq), :]
        vc = v_ref[0, :, pl.ds(start, tq), :]
        s = jnp.einsum('hqd,hkd->hqk', qh, kc, preferred_element_type=jnp.float32)
        s = s + cb[...] - crow[:, None, :]
        if diagonal:
            s = jnp.where((col <= row)[None], s, MASKED_SCORE)
        _online_softmax_step(s, vc, m_s, acc_s)

    def full_chunk(c, carry):
        attend_chunk(c, False)
        return carry

    lax.fori_loop(0, i, full_chunk, 0)
    attend_chunk(i, True)
    _write_normalised(acc_s, o_ref)


def fox_prompt(q, k, v, logf):
    B, S = q.shape[:2]
    tq = Q_BLOCK
    bf = jnp.bfloat16
    c = jnp.cumsum(logf, axis=1)
    qh = _head_major(q * HEAD_DIM ** -0.5).astype(bf)
    kh = _head_major(k).astype(bf)
    vh = _with_ones(_head_major(v)).astype(bf)
    o = pl.pallas_call(
        functools.partial(_fox_prompt_body, tq=tq),
        grid=(B, S // tq),
        in_specs=[
            pl.BlockSpec((1, H_FOX, tq, HEAD_DIM), lambda b, i: (b, 0, i, 0)),
            pl.BlockSpec((1, H_FOX, S, HEAD_DIM), lambda b, i: (b, 0, 0, 0)),
            pl.BlockSpec((1, H_FOX, S, 2 * HEAD_DIM), lambda b, i: (b, 0, 0, 0)),
            pl.BlockSpec((1, tq, H_FOX), lambda b, i: (b, i, 0)),
            pl.BlockSpec((1, H_FOX, S), lambda b, i: (b, 0, 0)),
        ],
        out_specs=pl.BlockSpec((1, H_FOX, tq, HEAD_DIM), lambda b, i: (b, 0, i, 0)),
        out_shape=jax.ShapeDtypeStruct((B, H_FOX, S, HEAD_DIM), jnp.float32),
        scratch_shapes=[
            pltpu.VMEM((H_FOX, tq, tq), jnp.float32),
            pltpu.VMEM((H_FOX, tq, tq), jnp.float32),
            pltpu.VMEM((H_FOX, tq, 2 * HEAD_DIM), jnp.float32),
        ],
        compiler_params=pltpu.CompilerParams(
            dimension_semantics=("arbitrary", "arbitrary"), vmem_limit_bytes=48 * 1024 * 1024),
        name="fox_prompt",
    )(qh, kh, vh, c, c.transpose(0, 2, 1))
    return _head_major(o)


PAGES_PER_STEP = 16
IDX_PAGES_PER_STEP = 16


def _paged_attn_body(pt_ref, qbd_ref, q_ref, rb_ref, pb_ref, kn_ref, vn_ref, bn_ref, *rest, g, pb_axis, nh, nt):
    k_refs, v_refs = rest[:g], rest[g:2 * g]
    if pb_axis == "logf":
        lf_refs = rest[2 * g:3 * g]
        o_ref, m_s, l_s, acc_s, carry_s = rest[3 * g:]
    else:
        o_ref, m_s, l_s, acc_s = rest[2 * g:]
    j = pl.program_id(1)
    rows = nh * nt
    bf = jnp.bfloat16

    @pl.when(j == 0)
    def _():
        m_s[...] = jnp.full(m_s.shape, RUNNING_MAX_INIT, jnp.float32)
        l_s[...] = jnp.zeros(l_s.shape, jnp.float32)
        acc_s[...] = jnp.zeros(acc_s.shape, jnp.float32)
        if pb_axis == "logf":
            carry_s[...] = jnp.zeros(carry_s.shape, jnp.float32)

    head_bias = [None] * g
    if pb_axis == "logf":
        carry = carry_s[...]
        for gi in reversed(range(g)):
            lf = lf_refs[gi][0]
            slots = lf.shape[1]
            lane = lax.broadcasted_iota(jnp.int32, lf.shape, 1)
            incl = lf
            sh = 1
            while sh < slots:
                incl = incl + jnp.where(lane + sh < slots, pltpu.roll(incl, slots - sh, axis=1), 0.0)
                sh *= 2
            head_bias[gi] = carry + (incl - lf)
            carry = carry + incl[:, 0:1]
        carry_s[...] = carry

    qbd = qbd_ref[0]
    rb = rb_ref[0]
    logits = []
    for gi in range(g):
        kp = k_refs[gi][0]
        slots = kp.shape[2]
        k2 = kp.reshape(kp.shape[0] * kp.shape[1], slots).astype(bf)
        s = jnp.dot(qbd, k2, preferred_element_type=jnp.float32)
        if pb_axis == "logf":
            pb = jnp.broadcast_to(head_bias[gi][:, None, :], (nh, nt, slots)).reshape(rows, slots)
        elif pb_axis == "head":
            pb = jnp.broadcast_to(pb_ref[0, gi][:, None, :], (nh, nt, slots)).reshape(rows, slots)
        else:
            pb = jnp.broadcast_to(pb_ref[0, gi][None], (nh, nt, slots)).reshape(rows, slots)
        logits.append(s + rb + pb)
    m_old = m_s[...]
    m_new = m_old
    for s in logits:
        m_new = jnp.maximum(m_new, jnp.max(s, axis=-1, keepdims=True))
    alpha = jnp.exp(m_old - m_new)
    l_new = alpha * l_s[...]
    acc_new = alpha * acc_s[...]
    for gi in range(g):
        vp = v_refs[gi][0]
        v2 = vp.reshape(vp.shape[0] * vp.shape[1], vp.shape[2]).astype(bf)
        p = jnp.exp(logits[gi] - m_new)
        l_new = l_new + jnp.sum(p, axis=-1, keepdims=True)
        acc_new = acc_new + lax.dot_general(p.astype(bf), v2, NT_DIMS, preferred_element_type=jnp.float32)
    m_s[...] = m_new
    l_s[...] = l_new
    acc_s[...] = acc_new

    @pl.when(j == pl.num_programs(1) - 1)
    def _():
        s = lax.dot_general(q_ref[0], kn_ref[0], NT_DIMS, preferred_element_type=jnp.float32) + bn_ref[0]
        m_prev = m_s[...]
        m_fin = jnp.maximum(m_prev, jnp.max(s, axis=-1, keepdims=True))
        scale_prev = jnp.exp(m_prev - m_fin)
        p = jnp.exp(s - m_fin)
        l_fin = scale_prev * l_s[...] + jnp.sum(p, axis=-1, keepdims=True)
        acc_fin = scale_prev * acc_s[...] + jnp.dot(p.astype(bf), vn_ref[0], preferred_element_type=jnp.float32)
        o_ref[0] = acc_fin / l_fin


def _slots_minor(cache):
    return jnp.moveaxis(cache, 1, -1)


def _block_diag_heads(x):
    B, R, H, D = x.shape
    eye = jnp.eye(H, dtype=x.dtype)
    return (x[:, :, :, None, :] * eye[None, None, :, :, None]).reshape(B, R, H, H * D)


def paged_attention(q, k_new, v_new, cache_k, cache_v, page_table, row_bias, page_bias, pb_axis, new_bias):
    B, T, H, D = q.shape
    n_pages = page_table.shape[1]
    slots = cache_k.shape[3]
    g = PAGES_PER_STEP
    n_steps = n_pages // g
    from_logf = pb_axis == "logf"
    bf = jnp.bfloat16
    qs = _head_major(q) * HEAD_DIM ** -0.5
    qr = qs.reshape(B, H * T, D).astype(bf)
    qbd = _block_diag_heads(qs.transpose(0, 2, 1, 3)).transpose(0, 2, 1, 3).reshape(B, H * T, H * D).astype(bf)
    kn = k_new.reshape(B, T * H, D).astype(bf)
    vn = _block_diag_heads(v_new).reshape(B, T * H, H * D).astype(bf)

    def step_pages(j):
        return (n_steps - 1 - j) if from_logf else j

    def page_map(gi, rank):
        return lambda b, j, pt: (pt[b, step_pages(j) * g + gi],) + (0,) * (rank - 1)

    per_batch = lambda b, j, pt: (b, 0, 0)
    page_spec = [pl.BlockSpec((1, H, D, slots), page_map(gi, 4)) for gi in range(g)]
    scratch = [pltpu.VMEM((H * T, 1), jnp.float32),
               pltpu.VMEM((H * T, 1), jnp.float32),
               pltpu.VMEM((H * T, H * D), jnp.float32)]
    if from_logf:
        pb_operand = jnp.zeros((1, 1, 1, slots), jnp.float32)
        pb_spec = pl.BlockSpec((1, 1, 1, slots), lambda b, j, pt: (0, 0, 0, 0))
        extra_specs = [pl.BlockSpec((1, H, slots), page_map(gi, 3)) for gi in range(g)]
        extra_operands = [page_bias] * g
        scratch.append(pltpu.VMEM((H, 1), jnp.float32))
    else:
        pb_operand = page_bias
        pb_spec = pl.BlockSpec((1, g, page_bias.shape[2], slots), lambda b, j, pt: (b, j, 0, 0))
        extra_specs, extra_operands = [], []
    grid_spec = pltpu.PrefetchScalarGridSpec(
        num_scalar_prefetch=1,
        grid=(B, n_steps),
        in_specs=[
            pl.BlockSpec((1, H * T, H * D), per_batch),
            pl.BlockSpec((1, H * T, D), per_batch),
            pl.BlockSpec((1, H * T, 1), per_batch),
            pb_spec,
            pl.BlockSpec((1, T * H, D), per_batch),
            pl.BlockSpec((1, T * H, H * D), per_batch),
            pl.BlockSpec((1, H * T, T * H), per_batch),
        ] + page_spec + page_spec + extra_specs,
        out_specs=pl.BlockSpec((1, H * T, H * D), per_batch),
        scratch_shapes=scratch,
    )
    o = pl.pallas_call(
        functools.partial(_paged_attn_body, g=g, pb_axis=pb_axis, nh=H, nt=T),
        grid_spec=grid_spec,
        out_shape=jax.ShapeDtypeStruct((B, H * T, H * D), jnp.float32),
        compiler_params=pltpu.CompilerParams(
            dimension_semantics=("arbitrary", "arbitrary"), vmem_limit_bytes=48 * 1024 * 1024),
        name="paged_attention",
    )(page_table, qbd, qr, row_bias, pb_operand, kn, vn, new_bias,
      *([cache_k] * g), *([cache_v] * g), *extra_operands)
    o = o.reshape(B, H, T, H, D)
    hh = jnp.arange(H)
    return o[:, hh, :, hh, :].transpose(1, 2, 0, 3)


def fox_sample(q, k_new, v_new, logf_new, cache_k, cache_v, cache_logf, page_table):
    B, T = q.shape[:2]
    cn = jnp.cumsum(logf_new, axis=1)
    row_bias = cn.transpose(0, 2, 1).reshape(B, H_FOX * T, 1)
    d = cn.transpose(0, 2, 1)[:, :, :, None] - cn.transpose(0, 2, 1)[:, :, None, :]
    tpos = jnp.arange(T)
    d = jnp.where(tpos[None, :] <= tpos[:, None], d, MASKED_SCORE)
    same = jnp.arange(H_FOX)[:, None] == jnp.arange(H_FOX)[None, :]
    nb = jnp.where(same[None, :, None, None, :], d[:, :, :, :, None], MASKED_SCORE)
    new_bias = nb.reshape(B, H_FOX * T, T * H_FOX)
    return paged_attention(q, k_new, v_new, _slots_minor(cache_k), _slots_minor(cache_v), page_table,
                           row_bias, _slots_minor(cache_logf.astype(jnp.float32)), "logf", new_bias)


def swiglu(x, wg, wu, wd):
    return (jax.nn.silu(x @ wg) * (x @ wu)) @ wd


TOKEN_TILE = 512


def _layer_norm_rows(z, g, b):
    mu = jnp.mean(z, axis=-1, keepdims=True)
    zc = z - mu
    var = jnp.mean(zc * zc, axis=-1, keepdims=True)
    return zc * lax.rsqrt(var + LN_EPS) * g + b


def _mix_body(heads_ref, x_ref, wo_ref, g_ref, b_ref, wr_ref, h_ref, s_ref):
    bf = jnp.bfloat16
    mix = jnp.dot(heads_ref[...].astype(bf), wo_ref[...], preferred_element_type=jnp.float32)
    h = _layer_norm_rows(DEEPNORM_ALPHA * x_ref[...] + mix, g_ref[...], b_ref[...])
    h_ref[...] = h
    logits = lax.dot_general(wr_ref[...], h.astype(bf), NT_DIMS, preferred_element_type=jnp.float32)
    s_ref[...] = 1.0 / (1.0 + jnp.exp(-logits))


def mix_ln_router(heads, x2, w_o, ln1_g, ln1_b, w_router):
    n, d = x2.shape
    tm = TOKEN_TILE
    bf = jnp.bfloat16
    row = lambda i: (i, 0)
    whole = lambda i: (0, 0)
    return pl.pallas_call(
        _mix_body,
        grid=(n // tm,),
        in_specs=[pl.BlockSpec((tm, d), row), pl.BlockSpec((tm, d), row),
                  pl.BlockSpec((d, d), whole), pl.BlockSpec((1, d), whole), pl.BlockSpec((1, d), whole),
                  pl.BlockSpec((N_EXPERTS, d), whole)],
        out_specs=[pl.BlockSpec((tm, d), row), pl.BlockSpec((N_EXPERTS, tm), lambda i: (0, i))],
        out_shape=[jax.ShapeDtypeStruct((n, d), jnp.float32),
                   jax.ShapeDtypeStruct((N_EXPERTS, n), jnp.float32)],
        compiler_params=pltpu.CompilerParams(
            dimension_semantics=("arbitrary",), vmem_limit_bytes=48 * 1024 * 1024),
        name="mix_ln_router",
    )(heads, x2, w_o.astype(bf), ln1_g[None], ln1_b[None], w_router.T.astype(bf))


def _final_body(h_ref, r_ref, wg_ref, wu_ref, wd_ref, g_ref, b_ref, y_ref):
    bf = jnp.bfloat16
    h = h_ref[...]
    hb = h.astype(bf)
    a = jnp.dot(hb, wg_ref[...], preferred_element_type=jnp.float32)
    u = jnp.dot(hb, wu_ref[...], preferred_element_type=jnp.float32)
    shared = jnp.dot(((a / (1.0 + jnp.exp(-a))) * u).astype(bf), wd_ref[...], preferred_element_type=jnp.float32)
    y_ref[...] = _layer_norm_rows(DEEPNORM_ALPHA * h + (r_ref[...] + shared), g_ref[...], b_ref[...])


def shared_ln_final(h, routed, w_gate_s, w_up_s, w_down_s, ln2_g, ln2_b):
    n, d = h.shape
    tm = TOKEN_TILE
    ds = w_gate_s.shape[1]
    bf = jnp.bfloat16
    row = lambda i: (i, 0)
    whole = lambda i: (0, 0)
    return pl.pallas_call(
        _final_body,
        grid=(n // tm,),
        in_specs=[pl.BlockSpec((tm, d), row), pl.BlockSpec((tm, d), row),
                  pl.BlockSpec((d, ds), whole), pl.BlockSpec((d, ds), whole), pl.BlockSpec((ds, d), whole),
                  pl.BlockSpec((1, d), whole), pl.BlockSpec((1, d), whole)],
        out_specs=pl.BlockSpec((tm, d), row),
        out_shape=jax.ShapeDtypeStruct((n, d), jnp.float32),
        compiler_params=pltpu.CompilerParams(
            dimension_semantics=("arbitrary",), vmem_limit_bytes=48 * 1024 * 1024),
        name="shared_ln_final",
    )(h, routed, w_gate_s.astype(bf), w_up_s.astype(bf), w_down_s.astype(bf), ln2_g[None], ln2_b[None])


def _take_first_max(v, rowid):
    m = jnp.max(v, axis=0, keepdims=True)
    first = jnp.min(jnp.where(v == m, rowid, float(v.shape[0])), axis=0, keepdims=True)
    return m, first, jnp.where(rowid == first, -jnp.inf, v)


def _route_body(s_ref, b_ref, idx_ref, gate_ref):
    scores = s_ref[...]
    ne, tm = scores.shape
    gsz = ne // N_GROUPS
    biased = scores + b_ref[...]
    rid_g = lax.broadcasted_iota(jnp.int32, (gsz, tm), 0).astype(jnp.float32)
    gscore = []
    for g in range(N_GROUPS):
        m1, _, rest = _take_first_max(biased[g * gsz:(g + 1) * gsz], rid_g)
        gscore.append(m1 + jnp.max(rest, axis=0, keepdims=True))
    gs = jnp.concatenate(gscore, axis=0)
    gid = lax.broadcasted_iota(jnp.int32, (N_GROUPS, tm), 0).astype(jnp.float32)
    keep = jnp.zeros((N_GROUPS, tm), jnp.float32)
    for _ in range(TOPK_GROUPS):
        _, first, gs_next = _take_first_max(gs, gid)
        keep = jnp.where(gid == first, 1.0, keep)
        gs = gs_next
    masked = jnp.concatenate(
        [jnp.where(keep[g:g + 1] > 0.0, biased[g * gsz:(g + 1) * gsz], -jnp.inf) for g in range(N_GROUPS)], axis=0)
    rid = lax.broadcasted_iota(jnp.int32, (ne, tm), 0).astype(jnp.float32)
    picks, weights = [], []
    for _ in range(TOP_K):
        _, first, masked = _take_first_max(masked, rid)
        picks.append(first)
        weights.append(jnp.sum(jnp.where(rid == first, scores, 0.0), axis=0, keepdims=True))
    w = jnp.concatenate(weights, axis=0)
    idx_ref[...] = jnp.concatenate(picks, axis=0).astype(jnp.int32)
    gate_ref[...] = w / jnp.sum(w, axis=0, keepdims=True) * ROUTED_SCALE


def route(scores_t, router_bias):
    ne, n = scores_t.shape
    tm = TOKEN_TILE
    idx_t, gate_t = pl.pallas_call(
        _route_body,
        grid=(n // tm,),
        in_specs=[pl.BlockSpec((ne, tm), lambda i: (0, i)), pl.BlockSpec((ne, 1), lambda i: (0, 0))],
        out_specs=[pl.BlockSpec((TOP_K, tm), lambda i: (0, i)), pl.BlockSpec((TOP_K, tm), lambda i: (0, i))],
        out_shape=[jax.ShapeDtypeStruct((TOP_K, n), jnp.int32), jax.ShapeDtypeStruct((TOP_K, n), jnp.float32)],
        compiler_params=pltpu.CompilerParams(dimension_semantics=("arbitrary",)),
        name="route_topk",
    )(scores_t, router_bias.astype(jnp.float32)[:, None])
    return idx_t.T, gate_t.T


RANK_TILE = 256


def _rank_body(idx_ref, rank_ref, count_ref, running):
    i = pl.program_id(0)
    tm, nk = idx_ref.shape

    @pl.when(i == 0)
    def _():
        running[...] = jnp.zeros(running.shape, jnp.float32)

    idx = idx_ref[...]
    lane = lax.broadcasted_iota(jnp.int32, (tm, N_EXPERTS), 1)
    hots = [jnp.where(lane == idx[:, k:k + 1], 1.0, 0.0) for k in range(nk)]
    onehot = hots[0]
    for k in range(1, nk):
        onehot = onehot + hots[k]
    r = lax.broadcasted_iota(jnp.int32, (tm, tm), 0)
    c = lax.broadcasted_iota(jnp.int32, (tm, tm), 1)
    earlier = jnp.where(c < r, 1.0, 0.0).astype(jnp.bfloat16)
    before = jnp.dot(earlier, onehot.astype(jnp.bfloat16), preferred_element_type=jnp.float32) + running[...]
    lane_k = lax.broadcasted_iota(jnp.int32, (tm, nk), 1)
    out = jnp.zeros((tm, nk), jnp.float32)
    for k in range(nk):
        out = jnp.where(lane_k == k, jnp.sum(hots[k] * before, axis=-1, keepdims=True), out)
    rank_ref[...] = out.astype(jnp.int32)
    running[...] = running[...] + jnp.sum(onehot, axis=0, keepdims=True)
    count_ref[...] = running[...]


def expert_rank(idx):
    n, nk = idx.shape
    tm = RANK_TILE
    rank, counts = pl.pallas_call(
        _rank_body,
        grid=(n // tm,),
        in_specs=[pl.BlockSpec((tm, nk), lambda i: (i, 0))],
        out_specs=[pl.BlockSpec((tm, nk), lambda i: (i, 0)),
                   pl.BlockSpec((1, N_EXPERTS), lambda i: (0, 0))],
        out_shape=[jax.ShapeDtypeStruct((n, nk), jnp.int32),
                   jax.ShapeDtypeStruct((1, N_EXPERTS), jnp.float32)],
        scratch_shapes=[pltpu.VMEM((1, N_EXPERTS), jnp.float32)],
        compiler_params=pltpu.CompilerParams(dimension_semantics=("arbitrary",)),
        name="expert_rank",
    )(idx)
    return rank, counts[0].astype(jnp.int32)


EXPERT_TILE = 256


def _expert_body(be_ref, nu_ref, x_ref, wg_ref, wu_ref, wd_ref, y_ref):
    i = pl.program_id(0)

    @pl.when(i < nu_ref[0])
    def _():
        bf = jnp.bfloat16
        x = x_ref[...].astype(bf)
        a = jnp.dot(x, wg_ref[0].astype(bf), preferred_element_type=jnp.float32)
        u = jnp.dot(x, wu_ref[0].astype(bf), preferred_element_type=jnp.float32)
        hmid = (a / (1.0 + jnp.exp(-a))) * u
        y_ref[...] = jnp.dot(hmid.astype(bf), wd_ref[0].astype(bf), preferred_element_type=jnp.float32)

    @pl.when(i >= nu_ref[0])
    def _():
        y_ref[...] = jnp.zeros(y_ref.shape, jnp.float32)


def expert_ffn(xb, tile_expert, n_used, w_gate_e, w_up_e, w_down_e):
    rows, d = xb.shape
    tb = EXPERT_TILE
    de = w_gate_e.shape[2]
    grid_spec = pltpu.PrefetchScalarGridSpec(
        num_scalar_prefetch=2,
        grid=(rows // tb,),
        in_specs=[
            pl.BlockSpec((tb, d), lambda i, te, nu: (i, 0)),
            pl.BlockSpec((1, d, de), lambda i, te, nu: (te[i], 0, 0)),
            pl.BlockSpec((1, d, de), lambda i, te, nu: (te[i], 0, 0)),
            pl.BlockSpec((1, de, d), lambda i, te, nu: (te[i], 0, 0)),
        ],
        out_specs=pl.BlockSpec((tb, d), lambda i, te, nu: (i, 0)),
    )
    return pl.pallas_call(
        _expert_body,
        grid_spec=grid_spec,
        out_shape=jax.ShapeDtypeStruct((rows, d), jnp.float32),
        compiler_params=pltpu.CompilerParams(
            dimension_semantics=("arbitrary",), vmem_limit_bytes=48 * 1024 * 1024),
        name="expert_ffn",
    )(tile_expert, n_used, xb, w_gate_e, w_up_e, w_down_e)


def routed_experts(x2, idx, gate, w_gate_e, w_up_e, w_down_e):
    n, d = x2.shape
    a = n * TOP_K
    n_tiles = (a + N_EXPERTS * (EXPERT_TILE - 1) + EXPERT_TILE - 1) // EXPERT_TILE
    rows = n_tiles * EXPERT_TILE
    rank, counts = expert_rank(idx.astype(jnp.int32))
    padded = (counts + EXPERT_TILE - 1) // EXPERT_TILE * EXPERT_TILE
    pad_end = jnp.cumsum(padded)
    pad_start = pad_end - padded
    dest2 = pad_start[idx] + rank
    dest = dest2.reshape(-1)
    token = jnp.repeat(jnp.arange(n, dtype=jnp.int32), TOP_K)
    row_token = jnp.full((rows,), n, jnp.int32).at[dest].set(token)
    tile_expert = jnp.minimum(
        jnp.searchsorted(pad_end, jnp.arange(n_tiles) * EXPERT_TILE, side='right'), N_EXPERTS - 1)
    n_used = (pad_end[-1:] // EXPERT_TILE).astype(jnp.int32)
    x_pad = jnp.concatenate([x2, jnp.zeros((1, d), x2.dtype)], axis=0)
    xb = x_pad[row_token]
    yb = expert_ffn(xb, tile_expert.astype(jnp.int32), n_used, w_gate_e, w_up_e, w_down_e)
    return (yb[dest2] * gate[:, :, None]).sum(axis=1)


def concat_heads(o_a, o_f):
    B, T = o_a.shape[:2]
    return jnp.concatenate([o_a.reshape(B * T, W_DSA), o_f.reshape(B * T, W_FOX)], axis=-1)


def finish_tokens(x2, heads, w_o, ln1_g, ln1_b, w_router, router_bias, w_gate_e, w_up_e, w_down_e,
                  w_gate_s, w_up_s, w_down_s, ln2_g, ln2_b):
    h, scores = mix_ln_router(heads, x2, w_o, ln1_g, ln1_b, w_router)
    idx, gate = route(scores, router_bias)
    routed = routed_experts(h, idx, gate, w_gate_e, w_up_e, w_down_e)
    return shared_ln_final(h, routed, w_gate_s, w_up_s, w_down_s, ln2_g, ln2_b)


def kernel(x_prompt, x_sample, cache_k_dsa, cache_v_dsa, cache_k_idx, cache_k_fox, cache_v_fox,
           cache_logf_fox, page_table, w_in, b_forget, idx_ln_g, idx_ln_b, w_o, ln1_g, ln1_b,
           w_router, router_bias, w_gate_e, w_up_e, w_down_e, w_gate_s, w_up_s, w_down_s,
           ln2_g, ln2_b):
    seq = x_prompt.shape[1]
    dec_seq = x_sample.shape[1]
    past = page_table.shape[1] * PAGE_SIZE
    pos_p = jnp.arange(seq, dtype=jnp.int32)
    pos_s = past + jnp.arange(dec_seq, dtype=jnp.int32)
    xp, xs = x_prompt, x_sample
    l = 0
    q_a, k_a, v_a, q_i, k_i, w_i, q_f, k_f, v_f, logf = project_mixers(
        xp, pos_p, w_in[l], b_forget[l], idx_ln_g[l], idx_ln_b[l])
    o_a = dsa_prompt(q_a, k_a, v_a, q_i, k_i, w_i)
    o_f = fox_prompt(q_f, k_f, v_f, logf)
    heads_p = concat_heads(o_a, o_f)
    outs_p = (k_a, v_a, k_i, k_f, v_f, logf.astype(x_prompt.dtype))
    q_a, k_a, v_a, q_i, k_i, w_i, q_f, k_f, v_f, logf = project_mixers(
        xs, pos_s, w_in[l], b_forget[l], idx_ln_g[l], idx_ln_b[l])
    o_a = dsa_sample(q_a, k_a, v_a, q_i, k_i, w_i, cache_k_dsa[l], cache_v_dsa[l], cache_k_idx[l], page_table)
    o_f = fox_sample(q_f, k_f, v_f, logf, cache_k_fox[l], cache_v_fox[l], cache_logf_fox[l], page_table)
    heads_s = concat_heads(o_a, o_f)
    outs_s = (k_a, v_a, k_i, k_f, v_f, logf.astype(x_sample.dtype))
    n_p = heads_p.shape[0]
    x2 = jnp.concatenate([xp.reshape(n_p, D_MODEL), xs.reshape(-1, D_MODEL)], axis=0)
    y = finish_tokens(x2, jnp.concatenate([heads_p, heads_s], axis=0), w_o[l], ln1_g[l], ln1_b[l],
                      w_router[l], router_bias[l], w_gate_e[l], w_up_e[l], w_down_e[l],
                      w_gate_s[l], w_up_s[l], w_down_s[l], ln2_g[l], ln2_b[l])
    yp = y[:n_p].reshape(x_prompt.shape)
    ys = y[n_p:].reshape(x_sample.shape)
    return (yp, ys) + tuple(o[None] for o in outs_p) + tuple(o[None] for o in outs_s)
```

```python
import functools
import jax
import jax.numpy as jnp
from jax import lax
import numpy as np
from jax.experimental import pallas as pl
from jax.experimental.pallas import tpu as pltpu

D_MODEL = 1024
PAGE_SIZE = 128
HEAD_DIM = 64
N_HEADS = D_MODEL // HEAD_DIM
H_DSA = N_HEADS // 2
H_FOX = N_HEADS - H_DSA
W_DSA = H_DSA * HEAD_DIM
W_FOX = H_FOX * HEAD_DIM
MIX_WIDTH = W_DSA + W_FOX
H_IDX = 8
D_IDX = 64
TOPK_MAX = 256
Q_BLOCK = 128
ROPE_THETA = 10000.0
N_EXPERTS = 256
TOP_K = 8
N_GROUPS = 8
TOPK_GROUPS = 4
ROUTED_SCALE = 2.5
LN_EPS = 1e-5
DEPTH = 1
DEEPNORM_ALPHA = (2 * DEPTH) ** 0.25

_IN_SIZES = (W_DSA, W_DSA, W_DSA, H_IDX * D_IDX, D_IDX, H_IDX, W_FOX, W_FOX, W_FOX, H_FOX)
IN_COLS = sum(_IN_SIZES)
SPLIT_POINTS = tuple(int(v) for v in np.cumsum(_IN_SIZES)[:-1])

LANES = 128


def _proj_body(x_ref, w_ref, o_ref):
    o_ref[...] = jnp.dot(x_ref[...].astype(jnp.bfloat16), w_ref[...],
                         preferred_element_type=jnp.float32)


def _project(x2, w):
    m, d = x2.shape
    c = w.shape[1]
    cp = (c + LANES - 1) // LANES * LANES
    wp = jnp.pad(w, ((0, 0), (0, cp - c))).astype(jnp.bfloat16)
    tm = 256
    out = pl.pallas_call(
        _proj_body,
        grid=(m // tm,),
        in_specs=[pl.BlockSpec((tm, d), lambda i: (i, 0)),
                  pl.BlockSpec((d, cp), lambda i: (0, 0))],
        out_specs=pl.BlockSpec((tm, cp), lambda i: (i, 0)),
        out_shape=jax.ShapeDtypeStruct((m, cp), jnp.float32),
        compiler_params=pltpu.CompilerParams(vmem_limit_bytes=48 * 1024 * 1024),
        name="in_proj",
    )(x2, wp)
    return out[:, :c]


def layer_norm(x, g, b):
    xf = x.astype(jnp.float32)
    mu = xf.mean(-1, keepdims=True)
    var = jnp.square(xf - mu).mean(-1, keepdims=True)
    return ((xf - mu) * lax.rsqrt(var + LN_EPS) * g.astype(jnp.float32) + b.astype(jnp.float32)).astype(x.dtype)


def rope(x, pos):
    half = x.shape[-1] // 2
    freqs = ROPE_THETA ** (-jnp.arange(half, dtype=jnp.float32) / half)
    ang = pos.astype(jnp.float32)[:, None] * freqs[None, :]
    cos = jnp.cos(ang)[None, :, None, :]
    sin = jnp.sin(ang)[None, :, None, :]
    xf = x.astype(jnp.float32)
    x1, x2 = xf[..., :half], xf[..., half:]
    return jnp.concatenate([x1 * cos - x2 * sin, x2 * cos + x1 * sin], axis=-1).astype(x.dtype)


def project_mixers(x, pos, w_in, b_forget, idx_ln_g, idx_ln_b):
    B, T, _ = x.shape
    h = _project(x.reshape(B * T, D_MODEL), w_in).reshape(B, T, IN_COLS)
    q_a, k_a, v_a, q_i, k_i, w_i, q_f, k_f, v_f, f_logit = jnp.split(h, SPLIT_POINTS, axis=-1)
    q_a = rope(q_a.reshape(B, T, H_DSA, HEAD_DIM), pos)
    k_a = rope(k_a.reshape(B, T, H_DSA, HEAD_DIM), pos)
    v_a = v_a.reshape(B, T, H_DSA, HEAD_DIM)
    q_i = rope(q_i.reshape(B, T, H_IDX, D_IDX), pos)
    k_i = rope(layer_norm(k_i, idx_ln_g, idx_ln_b)[:, :, None, :], pos)[:, :, 0, :]
    w_i = w_i.astype(jnp.float32) * (H_IDX ** -0.5 * D_IDX ** -0.5)
    q_f = q_f.reshape(B, T, H_FOX, HEAD_DIM)
    k_f = k_f.reshape(B, T, H_FOX, HEAD_DIM)
    v_f = v_f.reshape(B, T, H_FOX, HEAD_DIM)
    logf = jax.nn.log_sigmoid((f_logit + b_forget).astype(jnp.float32))
    return q_a, k_a, v_a, q_i, k_i, w_i, q_f, k_f, v_f, logf


INT_MIN = -2 ** 31
MASKED_SCORE = -1e30
RUNNING_MAX_INIT = -1e29
NT_DIMS = (((1,), (1,)), ((), ()))


def _head_major(x):
    return x.transpose(0, 2, 1, 3)


def _with_ones(v):
    return jnp.concatenate([v, jnp.ones_like(v)], axis=-1)


def _online_softmax_step(s, vc, m_ref, acc_ref):
    m_old = m_ref[...]
    m_new = jnp.maximum(m_old, jnp.max(s, axis=-1, keepdims=True))
    alpha = jnp.exp(m_old - m_new)
    p = jnp.exp(s - m_new)
    pv = jnp.einsum('hqk,hkd->hqd', p.astype(jnp.bfloat16), vc, preferred_element_type=jnp.float32)
    acc_ref[...] = alpha * acc_ref[...] + pv
    m_ref[...] = m_new


def _init_softmax_state(m_ref, acc_ref):
    m_ref[...] = jnp.full(m_ref.shape, RUNNING_MAX_INIT, jnp.float32)
    acc_ref[...] = jnp.zeros(acc_ref.shape, jnp.float32)


def _write_normalised(acc_ref, o_ref):
    a = acc_ref[...]
    o_ref[0] = a[:, :, :HEAD_DIM] / a[:, :, HEAD_DIM:]


def _kth_largest_key(count_ge, rows, kf, total):
    c0 = count_ge(jnp.zeros((rows, 1), jnp.int32))
    nonneg = c0 >= kf
    prefix = jnp.where(nonneg, 0, INT_MIN).astype(jnp.int32)
    cnt = jnp.where(nonneg, c0, total)

    def bit_step(t, carry):
        prefix, cnt = carry
        cand = prefix + lax.shift_left(jnp.int32(1), 30 - t)
        cc = count_ge(cand)
        take = cc >= kf
        return jnp.where(take, cand, prefix), jnp.where(take, cc, cnt)

    thr, cnt = lax.fori_loop(0, 31, bit_step, (prefix, cnt))
    thr = jnp.maximum(thr, INT_MIN + 1)
    excess = cnt > kf
    any_excess = jnp.max(jnp.where(excess, 1.0, 0.0)) > 0.0
    return thr, excess, any_excess


def _last_kept_tie(count_ties_before, need, rows, index_bits):
    def idx_step(t, last):
        cand = last + lax.shift_left(jnp.int32(1), index_bits - 1 - t)
        return jnp.where(count_ties_before(cand) < need, cand, last)

    return lax.fori_loop(0, index_bits, idx_step, jnp.zeros((rows, 1), jnp.int32))


def _dsa_prompt_body(qi_ref, ki_ref, wi_ref, q_ref, k_ref, v_ref, o_ref,
                     keys, wb, m_s, acc_s, *, topk, tq, seq_bits):
    i = pl.program_id(1)
    n_chunks = i + 1
    row = lax.broadcasted_iota(jnp.int32, (tq, tq), 0)
    col = lax.broadcasted_iota(jnp.int32, (tq, tq), 1)
    kf = float(topk)

    w = wi_ref[0]
    for h in range(H_IDX):
        wb[h] = jnp.broadcast_to(w[:, h:h + 1], (tq, tq))
    qi_all = qi_ref[0].reshape(H_IDX * tq, D_IDX)

    def index_chunk(c, carry):
        kic = ki_ref[0, pl.ds(pl.multiple_of(c * tq, tq), tq), :]
        s = lax.dot_general(qi_all, kic, NT_DIMS, preferred_element_type=jnp.float32)
        sc = jnp.maximum(s[0:tq], 0.0) * wb[0]
        for h in range(1, H_IDX):
            sc = sc + jnp.maximum(s[h * tq:(h + 1) * tq], 0.0) * wb[h]
        sc = jnp.where(sc == 0.0, 0.0, sc)
        bits = pltpu.bitcast(sc, jnp.int32)
        keys[c] = jnp.where(bits < 0, bits ^ 0x7FFFFFFF, bits)
        return carry

    lax.fori_loop(0, n_chunks, index_chunk, 0)
    keys[i] = jnp.where(col <= row, keys[i], INT_MIN)

    def count_ge(cand):
        cb = jnp.broadcast_to(cand, (tq, tq))

        def body(c, acc):
            return acc + jnp.where(keys[c] >= cb, 1.0, 0.0)

        acc = lax.fori_loop(0, n_chunks, body, jnp.zeros((tq, tq), jnp.float32))
        return jnp.sum(acc, axis=-1, keepdims=True)

    thr, excess, any_excess = _kth_largest_key(count_ge, tq, kf, (n_chunks * tq).astype(jnp.float32))

    @pl.when(any_excess)
    def _():
        need = kf - count_ge(thr + 1)
        thb = jnp.broadcast_to(thr, (tq, tq))

        def count_ties_before(j):
            jb = jnp.broadcast_to(j, (tq, tq))

            def body(c, acc):
                hit = (keys[c] == thb) & ((col + c * tq) < jb)
                return acc + jnp.where(hit, 1.0, 0.0)

            acc = lax.fori_loop(0, n_chunks, body, jnp.zeros((tq, tq), jnp.float32))
            return jnp.sum(acc, axis=-1, keepdims=True)

        last = _last_kept_tie(count_ties_before, need, tq, seq_bits)
        lb = jnp.broadcast_to(last, (tq, tq))
        eb = jnp.broadcast_to(excess, (tq, tq))

        def drop(c, carry):
            kk = keys[c]
            keys[c] = jnp.where(eb & (kk == thb) & ((col + c * tq) > lb), INT_MIN, kk)
            return carry

        lax.fori_loop(0, n_chunks, drop, 0)

    thb = jnp.broadcast_to(thr, (tq, tq))
    _init_softmax_state(m_s, acc_s)
    qh = q_ref[0]

    def attend_chunk(c, carry):
        sel = keys[c] >= thb
        start = pl.multiple_of(c * tq, tq)
        kc = k_ref[0, :, pl.ds(start, tq), :]
        vc = v_ref[0, :, pl.ds(start, tq), :]
        s = jnp.einsum('hqd,hkd->hqk', qh, kc, preferred_element_type=jnp.float32)
        s = jnp.where(sel[None], s, MASKED_SCORE)
        _online_softmax_step(s, vc, m_s, acc_s)
        return carry

    lax.fori_loop(0, n_chunks, attend_chunk, 0)
    _write_normalised(acc_s, o_ref)


def dsa_prompt(q, k, v, q_i, k_i, w_i):
    B, S = q.shape[:2]
    topk = min(TOPK_MAX, S // 4)
    tq = Q_BLOCK
    nq = S // tq
    bf = jnp.bfloat16
    qh = _head_major(q * HEAD_DIM ** -0.5).astype(bf)
    kh = _head_major(k).astype(bf)
    vh = _with_ones(_head_major(v)).astype(bf)
    qih = _head_major(q_i).astype(bf)
    body = functools.partial(_dsa_prompt_body, topk=topk, tq=tq, seq_bits=max(1, (S - 1).bit_length()))
    o = pl.pallas_call(
        body,
        grid=(B, nq),
        in_specs=[
            pl.BlockSpec((1, H_IDX, tq, D_IDX), lambda b, i: (b, 0, i, 0)),
            pl.BlockSpec((1, S, D_IDX), lambda b, i: (b, 0, 0)),
            pl.BlockSpec((1, tq, H_IDX), lambda b, i: (b, i, 0)),
            pl.BlockSpec((1, H_DSA, tq, HEAD_DIM), lambda b, i: (b, 0, i, 0)),
            pl.BlockSpec((1, H_DSA, S, HEAD_DIM), lambda b, i: (b, 0, 0, 0)),
            pl.BlockSpec((1, H_DSA, S, 2 * HEAD_DIM), lambda b, i: (b, 0, 0, 0)),
        ],
        out_specs=pl.BlockSpec((1, H_DSA, tq, HEAD_DIM), lambda b, i: (b, 0, i, 0)),
        out_shape=jax.ShapeDtypeStruct((B, H_DSA, S, HEAD_DIM), jnp.float32),
        scratch_shapes=[
            pltpu.VMEM((nq, tq, tq), jnp.int32),
            pltpu.VMEM((H_IDX, tq, tq), jnp.float32),
            pltpu.VMEM((H_DSA, tq, tq), jnp.float32),
            pltpu.VMEM((H_DSA, tq, 2 * HEAD_DIM), jnp.float32),
        ],
        compiler_params=pltpu.CompilerParams(
            dimension_semantics=("arbitrary", "arbitrary"), vmem_limit_bytes=48 * 1024 * 1024),
        name="dsa_prompt",
    )(qih, k_i.astype(bf), w_i, qh, kh, vh)
    return _head_major(o)


def _score_keys(s, wcol, nt):
    sc = jnp.maximum(s[0:nt], 0.0) * wcol[0:nt]
    for h in range(1, H_IDX):
        sc = sc + jnp.maximum(s[h * nt:(h + 1) * nt], 0.0) * wcol[h * nt:(h + 1) * nt]
    sc = jnp.where(sc == 0.0, 0.0, sc)
    bits = pltpu.bitcast(sc, jnp.int32)
    return jnp.where(bits < 0, bits ^ 0x7FFFFFFF, bits)


def _idx_sample_body(pt_ref, qi_ref, wc_ref, kin_ref, *rest, g, n_pages, nt, topk, index_bits):
    ki_refs = rest[:g]
    pb_ref, sn_ref, keys = rest[g:]
    j = pl.program_id(1)
    bf = jnp.bfloat16
    slots = keys.shape[2]
    qi = qi_ref[0]
    wcol = wc_ref[0]
    for gi in range(g):
        s = jnp.dot(qi, ki_refs[gi][0].astype(bf), preferred_element_type=jnp.float32)
        keys[j * g + gi] = _score_keys(s, wcol, nt)

    @pl.when(j == pl.num_programs(1) - 1)
    def _():
        row = lax.broadcasted_iota(jnp.int32, (nt, slots), 0)
        col = lax.broadcasted_iota(jnp.int32, (nt, slots), 1)
        s = lax.dot_general(qi, kin_ref[0], NT_DIMS, preferred_element_type=jnp.float32)
        keys[n_pages] = jnp.where(col <= row, _score_keys(s, wcol, nt), INT_MIN)
        kf = float(topk)

        def count_ge(cand):
            hit = jnp.where(keys[...] >= cand[None], 1.0, 0.0)
            return jnp.sum(jnp.sum(hit, axis=0), axis=-1, keepdims=True)

        total = jnp.float32((n_pages + 1) * slots)
        thr, excess, any_excess = _kth_largest_key(count_ge, nt, kf, total)

        @pl.when(any_excess)
        def _():
            need = kf - count_ge(thr + 1)
            pos = (lax.broadcasted_iota(jnp.int32, keys.shape, 0) * slots
                   + lax.broadcasted_iota(jnp.int32, keys.shape, 2))

            def count_ties_before(jj):
                hit = jnp.where((keys[...] == thr[None]) & (pos < jj[None]), 1.0, 0.0)
                return jnp.sum(jnp.sum(hit, axis=0), axis=-1, keepdims=True)

            last = _last_kept_tie(count_ties_before, need, nt, index_bits)
            kk = keys[...]
            keys[...] = jnp.where(excess[None] & (kk == thr[None]) & (pos > last[None]), INT_MIN, kk)

        bias = jnp.where(keys[...] >= thr[None], 0.0, MASKED_SCORE)
        sn_ref[0] = bias[n_pages]
        pb_ref[0] = bias[:n_pages]


def dsa_sample(q, k_new, v_new, q_i, k_i_new, w_i, cache_k, cache_v, cache_ki, page_table):
    B, T, H, D = q.shape
    n_pages = page_table.shape[1]
    slots = cache_ki.shape[1]
    L = n_pages * slots + T
    topk = min(TOPK_MAX, L // 4)
    g = IDX_PAGES_PER_STEP
    bf = jnp.bfloat16
    qi = _head_major(q_i).reshape(B, H_IDX * T, D_IDX).astype(bf)
    wcol = w_i.transpose(0, 2, 1).reshape(B, H_IDX * T, 1)
    ki_new = jnp.pad(k_i_new, ((0, 0), (0, slots - T), (0, 0))).astype(bf)
    cache_kit = _slots_minor(cache_ki)

    def page_map(gi):
        return lambda b, j, pt: (pt[b, j * g + gi], 0, 0)

    per_batch = lambda b, j, pt: (b, 0, 0)
    grid_spec = pltpu.PrefetchScalarGridSpec(
        num_scalar_prefetch=1,
        grid=(B, n_pages // g),
        in_specs=[
            pl.BlockSpec((1, H_IDX * T, D_IDX), per_batch),
            pl.BlockSpec((1, H_IDX * T, 1), per_batch),
            pl.BlockSpec((1, slots, D_IDX), per_batch),
        ] + [pl.BlockSpec((1, D_IDX, slots), page_map(gi)) for gi in range(g)],
        out_specs=[
            pl.BlockSpec((1, n_pages, T, slots), lambda b, j, pt: (b, 0, 0, 0)),
            pl.BlockSpec((1, T, slots), per_batch),
        ],
        scratch_shapes=[pltpu.VMEM((n_pages + 1, T, slots), jnp.int32)],
    )
    page_bias, bias_new = pl.pallas_call(
        functools.partial(_idx_sample_body, g=g, n_pages=n_pages, nt=T, topk=topk,
                          index_bits=max(1, ((n_pages + 1) * slots - 1).bit_length())),
        grid_spec=grid_spec,
        out_shape=[jax.ShapeDtypeStruct((B, n_pages, T, slots), jnp.float32),
                   jax.ShapeDtypeStruct((B, T, slots), jnp.float32)],
        compiler_params=pltpu.CompilerParams(
            dimension_semantics=("arbitrary", "arbitrary"), vmem_limit_bytes=48 * 1024 * 1024),
        name="idx_sample",
    )(page_table, qi, wcol, ki_new, *([cache_kit] * g))
    sel = bias_new[:, :, :T] == 0.0
    same = jnp.arange(H)[:, None] == jnp.arange(H)[None, :]
    nb = jnp.where(same[None, :, None, None, :] & sel[:, None, :, :, None], 0.0, MASKED_SCORE)
    new_bias = nb.reshape(B, H * T, T * H).astype(jnp.float32)
    row_bias = jnp.zeros((B, H * T, 1), jnp.float32)
    return paged_attention(q, k_new, v_new, _slots_minor(cache_k), _slots_minor(cache_v), page_table,
                           row_bias, page_bias, "token", new_bias)


def _fox_prompt_body(q_ref, k_ref, v_ref, ccol_ref, crow_ref, o_ref, cb, m_s, acc_s, *, tq):
    i = pl.program_id(1)
    row = lax.broadcasted_iota(jnp.int32, (tq, tq), 0)
    col = lax.broadcasted_iota(jnp.int32, (tq, tq), 1)
    ccol = ccol_ref[0]
    for h in range(H_FOX):
        cb[h] = jnp.broadcast_to(ccol[:, h:h + 1], (tq, tq))
    _init_softmax_state(m_s, acc_s)
    qh = q_ref[0]

    def attend_chunk(c, diagonal):
        start = pl.multiple_of(c * tq, tq)
        crow = crow_ref[0, :, pl.ds(start, tq)]
        kc = k_ref[0, :, pl.ds(start, tq), :]
        vc = v_ref[0, :, pl.ds(start, tq), :]
        s = jnp.einsum('hqd,hkd->hqk', qh, kc, preferred_element_type=jnp.float32)
        s = s + cb[...] - crow[:, None, :]
        if diagonal:
            s = jnp.where((col <= row)[None], s, MASKED_SCORE)
        _online_softmax_step(s, vc, m_s, acc_s)

    def full_chunk(c, carry):
        attend_chunk(c, False)
        return carry

    lax.fori_loop(0, i, full_chunk, 0)
    attend_chunk(i, True)
    _write_normalised(acc_s, o_ref)


def fox_prompt(q, k, v, logf):
    B, S = q.shape[:2]
    tq = Q_BLOCK
    bf = jnp.bfloat16
    c = jnp.cumsum(logf, axis=1)
    qh = _head_major(q * HEAD_DIM ** -0.5).astype(bf)
    kh = _head_major(k).astype(bf)
    vh = _with_ones(_head_major(v)).astype(bf)
    o = pl.pallas_call(
        functools.partial(_fox_prompt_body, tq=tq),
        grid=(B, S // tq),
        in_specs=[
            pl.BlockSpec((1, H_FOX, tq, HEAD_DIM), lambda b, i: (b, 0, i, 0)),
            pl.BlockSpec((1, H_FOX, S, HEAD_DIM), lambda b, i: (b, 0, 0, 0)),
            pl.BlockSpec((1, H_FOX, S, 2 * HEAD_DIM), lambda b, i: (b, 0, 0, 0)),
            pl.BlockSpec((1, tq, H_FOX), lambda b, i: (b, i, 0)),
            pl.BlockSpec((1, H_FOX, S), lambda b, i: (b, 0, 0)),
        ],
        out_specs=pl.BlockSpec((1, H_FOX, tq, HEAD_DIM), lambda b, i: (b, 0, i, 0)),
        out_shape=jax.ShapeDtypeStruct((B, H_FOX, S, HEAD_DIM), jnp.float32),
        scratch_shapes=[
            pltpu.VMEM((H_FOX, tq, tq), jnp.float32),
            pltpu.VMEM((H_FOX, tq, tq), jnp.float32),
            pltpu.VMEM((H_FOX, tq, 2 * HEAD_DIM), jnp.float32),
        ],
        compiler_params=pltpu.CompilerParams(
            dimension_semantics=("arbitrary", "arbitrary"), vmem_limit_bytes=48 * 1024 * 1024),
        name="fox_prompt",
    )(qh, kh, vh, c, c.transpose(0, 2, 1))
    return _head_major(o)


PAGES_PER_STEP = 16
IDX_PAGES_PER_STEP = 16


def _paged_attn_body(pt_ref, qbd_ref, q_ref, rb_ref, pb_ref, kn_ref, vn_ref, bn_ref, *rest, g, pb_axis, nh, nt):
    k_refs, v_refs = rest[:g], rest[g:2 * g]
    if pb_axis == "logf":
        lf_refs = rest[2 * g:3 * g]
        o_ref, m_s, l_s, acc_s, carry_s = rest[3 * g:]
    else:
        o_ref, m_s, l_s, acc_s = rest[2 * g:]
    j = pl.program_id(1)
    rows = nh * nt
    bf = jnp.bfloat16

    @pl.when(j == 0)
    def _():
        m_s[...] = jnp.full(m_s.shape, RUNNING_MAX_INIT, jnp.float32)
        l_s[...] = jnp.zeros(l_s.shape, jnp.float32)
        acc_s[...] = jnp.zeros(acc_s.shape, jnp.float32)
        if pb_axis == "logf":
            carry_s[...] = jnp.zeros(carry_s.shape, jnp.float32)

    head_bias = [None] * g
    if pb_axis == "logf":
        carry = carry_s[...]
        for gi in reversed(range(g)):
            lf = lf_refs[gi][0]
            slots = lf.shape[1]
            lane = lax.broadcasted_iota(jnp.int32, lf.shape, 1)
            incl = lf
            sh = 1
            while sh < slots:
                incl = incl + jnp.where(lane + sh < slots, pltpu.roll(incl, slots - sh, axis=1), 0.0)
                sh *= 2
            head_bias[gi] = carry + (incl - lf)
            carry = carry + incl[:, 0:1]
        carry_s[...] = carry

    qbd = qbd_ref[0]
    rb = rb_ref[0]
    logits = []
    for gi in range(g):
        kp = k_refs[gi][0]
        slots = kp.shape[2]
        k2 = kp.reshape(kp.shape[0] * kp.shape[1], slots).astype(bf)
        s = jnp.dot(qbd, k2, preferred_element_type=jnp.float32)
        if pb_axis == "logf":
            pb = jnp.broadcast_to(head_bias[gi][:, None, :], (nh, nt, slots)).reshape(rows, slots)
        elif pb_axis == "head":
            pb = jnp.broadcast_to(pb_ref[0, gi][:, None, :], (nh, nt, slots)).reshape(rows, slots)
        else:
            pb = jnp.broadcast_to(pb_ref[0, gi][None], (nh, nt, slots)).reshape(rows, slots)
        logits.append(s + rb + pb)
    m_old = m_s[...]
    m_new = m_old
    for s in logits:
        m_new = jnp.maximum(m_new, jnp.max(s, axis=-1, keepdims=True))
    alpha = jnp.exp(m_old - m_new)
    l_new = alpha * l_s[...]
    acc_new = alpha * acc_s[...]
    for gi in range(g):
        vp = v_refs[gi][0]
        v2 = vp.reshape(vp.shape[0] * vp.shape[1], vp.shape[2]).astype(bf)
        p = jnp.exp(logits[gi] - m_new)
        l_new = l_new + jnp.sum(p, axis=-1, keepdims=True)
        acc_new = acc_new + lax.dot_general(p.astype(bf), v2, NT_DIMS, preferred_element_type=jnp.float32)
    m_s[...] = m_new
    l_s[...] = l_new
    acc_s[...] = acc_new

    @pl.when(j == pl.num_programs(1) - 1)
    def _():
        s = lax.dot_general(q_ref[0], kn_ref[0], NT_DIMS, preferred_element_type=jnp.float32) + bn_ref[0]
        m_prev = m_s[...]
        m_fin = jnp.maximum(m_prev, jnp.max(s, axis=-1, keepdims=True))
        scale_prev = jnp.exp(m_prev - m_fin)
        p = jnp.exp(s - m_fin)
        l_fin = scale_prev * l_s[...] + jnp.sum(p, axis=-1, keepdims=True)
        acc_fin = scale_prev * acc_s[...] + jnp.dot(p.astype(bf), vn_ref[0], preferred_element_type=jnp.float32)
        o_ref[0] = acc_fin / l_fin


def _slots_minor(cache):
    return jnp.moveaxis(cache, 1, -1)


def _block_diag_heads(x):
    B, R, H, D = x.shape
    eye = jnp.eye(H, dtype=x.dtype)
    return (x[:, :, :, None, :] * eye[None, None, :, :, None]).reshape(B, R, H, H * D)


def paged_attention(q, k_new, v_new, cache_k, cache_v, page_table, row_bias, page_bias, pb_axis, new_bias):
    B, T, H, D = q.shape
    n_pages = page_table.shape[1]
    slots = cache_k.shape[3]
    g = PAGES_PER_STEP
    n_steps = n_pages // g
    from_logf = pb_axis == "logf"
    bf = jnp.bfloat16
    qs = _head_major(q) * HEAD_DIM ** -0.5
    qr = qs.reshape(B, H * T, D).astype(bf)
    qbd = _block_diag_heads(qs.transpose(0, 2, 1, 3)).transpose(0, 2, 1, 3).reshape(B, H * T, H * D).astype(bf)
    kn = k_new.reshape(B, T * H, D).astype(bf)
    vn = _block_diag_heads(v_new).reshape(B, T * H, H * D).astype(bf)

    def step_pages(j):
        return (n_steps - 1 - j) if from_logf else j

    def page_map(gi, rank):
        return lambda b, j, pt: (pt[b, step_pages(j) * g + gi],) + (0,) * (rank - 1)

    per_batch = lambda b, j, pt: (b, 0, 0)
    page_spec = [pl.BlockSpec((1, H, D, slots), page_map(gi, 4)) for gi in range(g)]
    scratch = [pltpu.VMEM((H * T, 1), jnp.float32),
               pltpu.VMEM((H * T, 1), jnp.float32),
               pltpu.VMEM((H * T, H * D), jnp.float32)]
    if from_logf:
        pb_operand = jnp.zeros((1, 1, 1, slots), jnp.float32)
        pb_spec = pl.BlockSpec((1, 1, 1, slots), lambda b, j, pt: (0, 0, 0, 0))
        extra_specs = [pl.BlockSpec((1, H, slots), page_map(gi, 3)) for gi in range(g)]
        extra_operands = [page_bias] * g
        scratch.append(pltpu.VMEM((H, 1), jnp.float32))
    else:
        pb_operand = page_bias
        pb_spec = pl.BlockSpec((1, g, page_bias.shape[2], slots), lambda b, j, pt: (b, j, 0, 0))
        extra_specs, extra_operands = [], []
    grid_spec = pltpu.PrefetchScalarGridSpec(
        num_scalar_prefetch=1,
        grid=(B, n_steps),
        in_specs=[
            pl.BlockSpec((1, H * T, H * D), per_batch),
            pl.BlockSpec((1, H * T, D), per_batch),
            pl.BlockSpec((1, H * T, 1), per_batch),
            pb_spec,
            pl.BlockSpec((1, T * H, D), per_batch),
            pl.BlockSpec((1, T * H, H * D), per_batch),
            pl.BlockSpec((1, H * T, T * H), per_batch),
        ] + page_spec + page_spec + extra_specs,
        out_specs=pl.BlockSpec((1, H * T, H * D), per_batch),
        scratch_shapes=scratch,
    )
    o = pl.pallas_call(
        functools.partial(_paged_attn_body, g=g, pb_axis=pb_axis, nh=H, nt=T),
        grid_spec=grid_spec,
        out_shape=jax.ShapeDtypeStruct((B, H * T, H * D), jnp.float32),
        compiler_params=pltpu.CompilerParams(
            dimension_semantics=("arbitrary", "arbitrary"), vmem_limit_bytes=48 * 1024 * 1024),
        name="paged_attention",
    )(page_table, qbd, qr, row_bias, pb_operand, kn, vn, new_bias,
      *([cache_k] * g), *([cache_v] * g), *extra_operands)
    o = o.reshape(B, H, T, H, D)
    hh = jnp.arange(H)
    return o[:, hh, :, hh, :].transpose(1, 2, 0, 3)


def fox_sample(q, k_new, v_new, logf_new, cache_k, cache_v, cache_logf, page_table):
    B, T = q.shape[:2]
    cn = jnp.cumsum(logf_new, axis=1)
    row_bias = cn.transpose(0, 2, 1).reshape(B, H_FOX * T, 1)
    d = cn.transpose(0, 2, 1)[:, :, :, None] - cn.transpose(0, 2, 1)[:, :, None, :]
    tpos = jnp.arange(T)
    d = jnp.where(tpos[None, :] <= tpos[:, None], d, MASKED_SCORE)
    same = jnp.arange(H_FOX)[:, None] == jnp.arange(H_FOX)[None, :]
    nb = jnp.where(same[None, :, None, None, :], d[:, :, :, :, None], MASKED_SCORE)
    new_bias = nb.reshape(B, H_FOX * T, T * H_FOX)
    return paged_attention(q, k_new, v_new, _slots_minor(cache_k), _slots_minor(cache_v), page_table,
                           row_bias, _slots_minor(cache_logf.astype(jnp.float32)), "logf", new_bias)


TOKEN_TILE = 512


def _layer_norm_rows(z, g, b):
    mu = jnp.mean(z, axis=-1, keepdims=True)
    zc = z - mu
    var = jnp.mean(zc * zc, axis=-1, keepdims=True)
    return zc * lax.rsqrt(var + LN_EPS) * g + b


def _mix_body(heads_ref, x_ref, wo_ref, g_ref, b_ref, wr_ref, h_ref, s_ref):
    bf = jnp.bfloat16
    mix = jnp.dot(heads_ref[...].astype(bf), wo_ref[...], preferred_element_type=jnp.float32)
    h = _layer_norm_rows(DEEPNORM_ALPHA * x_ref[...] + mix, g_ref[...], b_ref[...])
    h_ref[...] = h
    logits = lax.dot_general(wr_ref[...], h.astype(bf), NT_DIMS, preferred_element_type=jnp.float32)
    s_ref[...] = 1.0 / (1.0 + jnp.exp(-logits))


def mix_ln_router(heads, x2, w_o, ln1_g, ln1_b, w_router):
    n, d = x2.shape
    tm = TOKEN_TILE
    bf = jnp.bfloat16
    row = lambda i: (i, 0)
    whole = lambda i: (0, 0)
    return pl.pallas_call(
        _mix_body,
        grid=(n // tm,),
        in_specs=[pl.BlockSpec((tm, d), row), pl.BlockSpec((tm, d), row),
                  pl.BlockSpec((d, d), whole), pl.BlockSpec((1, d), whole), pl.BlockSpec((1, d), whole),
                  pl.BlockSpec((N_EXPERTS, d), whole)],
        out_specs=[pl.BlockSpec((tm, d), row), pl.BlockSpec((N_EXPERTS, tm), lambda i: (0, i))],
        out_shape=[jax.ShapeDtypeStruct((n, d), jnp.float32),
                   jax.ShapeDtypeStruct((N_EXPERTS, n), jnp.float32)],
        compiler_params=pltpu.CompilerParams(
            dimension_semantics=("arbitrary",), vmem_limit_bytes=48 * 1024 * 1024),
        name="mix_ln_router",
    )(heads, x2, w_o.astype(bf), ln1_g[None], ln1_b[None], w_router.T.astype(bf))


def _final_body(h_ref, r_ref, wg_ref, wu_ref, wd_ref, g_ref, b_ref, y_ref):
    bf = jnp.bfloat16
    h = h_ref[...]
    hb = h.astype(bf)
    a = jnp.dot(hb, wg_ref[...], preferred_element_type=jnp.float32)
    u = jnp.dot(hb, wu_ref[...], preferred_element_type=jnp.float32)
    shared = jnp.dot(((a / (1.0 + jnp.exp(-a))) * u).astype(bf), wd_ref[...], preferred_element_type=jnp.float32)
    y_ref[...] = _layer_norm_rows(DEEPNORM_ALPHA * h + (r_ref[...] + shared), g_ref[...], b_ref[...])


def shared_ln_final(h, routed, w_gate_s, w_up_s, w_down_s, ln2_g, ln2_b):
    n, d = h.shape
    tm = TOKEN_TILE
    ds = w_gate_s.shape[1]
    bf = jnp.bfloat16
    row = lambda i: (i, 0)
    whole = lambda i: (0, 0)
    return pl.pallas_call(
        _final_body,
        grid=(n // tm,),
        in_specs=[pl.BlockSpec((tm, d), row), pl.BlockSpec((tm, d), row),
                  pl.BlockSpec((d, ds), whole), pl.BlockSpec((d, ds), whole), pl.BlockSpec((ds, d), whole),
                  pl.BlockSpec((1, d), whole), pl.BlockSpec((1, d), whole)],
        out_specs=pl.BlockSpec((tm, d), row),
        out_shape=jax.ShapeDtypeStruct((n, d), jnp.float32),
        compiler_params=pltpu.CompilerParams(
            dimension_semantics=("arbitrary",), vmem_limit_bytes=48 * 1024 * 1024),
        name="shared_ln_final",
    )(h, routed, w_gate_s.astype(bf), w_up_s.astype(bf), w_down_s.astype(bf), ln2_g[None], ln2_b[None])


def _take_first_max(v, rowid):
    m = jnp.max(v, axis=0, keepdims=True)
    first = jnp.min(jnp.where(v == m, rowid, float(v.shape[0])), axis=0, keepdims=True)
    return m, first, jnp.where(rowid == first, -jnp.inf, v)


def _route_body(s_ref, b_ref, idx_ref, gate_ref):
    scores = s_ref[...]
    ne, tm = scores.shape
    gsz = ne // N_GROUPS
    biased = scores + b_ref[...]
    rid_g = lax.broadcasted_iota(jnp.int32, (gsz, tm), 0).astype(jnp.float32)
    gscore = []
    for g in range(N_GROUPS):
        m1, _, rest = _take_first_max(biased[g * gsz:(g + 1) * gsz], rid_g)
        gscore.append(m1 + jnp.max(rest, axis=0, keepdims=True))
    gs = jnp.concatenate(gscore, axis=0)
    gid = lax.broadcasted_iota(jnp.int32, (N_GROUPS, tm), 0).astype(jnp.float32)
    keep = jnp.zeros((N_GROUPS, tm), jnp.float32)
    for _ in range(TOPK_GROUPS):
        _, first, gs_next = _take_first_max(gs, gid)
        keep = jnp.where(gid == first, 1.0, keep)
        gs = gs_next
    masked = jnp.concatenate(
        [jnp.where(keep[g:g + 1] > 0.0, biased[g * gsz:(g + 1) * gsz], -jnp.inf) for g in range(N_GROUPS)], axis=0)
    rid = lax.broadcasted_iota(jnp.int32, (ne, tm), 0).astype(jnp.float32)
    picks, weights = [], []
    for _ in range(TOP_K):
        _, first, masked = _take_first_max(masked, rid)
        picks.append(first)
        weights.append(jnp.sum(jnp.where(rid == first, scores, 0.0), axis=0, keepdims=True))
    w = jnp.concatenate(weights, axis=0)
    idx_ref[...] = jnp.concatenate(picks, axis=0).astype(jnp.int32)
    gate_ref[...] = w / jnp.sum(w, axis=0, keepdims=True) * ROUTED_SCALE


def route(scores_t, router_bias):
    ne, n = scores_t.shape
    tm = TOKEN_TILE
    idx_t, gate_t = pl.pallas_call(
        _route_body,
        grid=(n // tm,),
        in_specs=[pl.BlockSpec((ne, tm), lambda i: (0, i)), pl.BlockSpec((ne, 1), lambda i: (0, 0))],
        out_specs=[pl.BlockSpec((TOP_K, tm), lambda i: (0, i)), pl.BlockSpec((TOP_K, tm), lambda i: (0, i))],
        out_shape=[jax.ShapeDtypeStruct((TOP_K, n), jnp.int32), jax.ShapeDtypeStruct((TOP_K, n), jnp.float32)],
        compiler_params=pltpu.CompilerParams(dimension_semantics=("arbitrary",)),
        name="route_topk",
    )(scores_t, router_bias.astype(jnp.float32)[:, None])
    return idx_t.T, gate_t.T


RANK_TILE = 256


def _rank_body(idx_ref, rank_ref, count_ref, running):
    i = pl.program_id(0)
    tm, nk = idx_ref.shape

    @pl.when(i == 0)
    def _():
        running[...] = jnp.zeros(running.shape, jnp.float32)

    idx = idx_ref[...]
    lane = lax.broadcasted_iota(jnp.int32, (tm, N_EXPERTS), 1)
    hots = [jnp.where(lane == idx[:, k:k + 1], 1.0, 0.0) for k in range(nk)]
    onehot = hots[0]
    for k in range(1, nk):
        onehot = onehot + hots[k]
    r = lax.broadcasted_iota(jnp.int32, (tm, tm), 0)
    c = lax.broadcasted_iota(jnp.int32, (tm, tm), 1)
    earlier = jnp.where(c < r, 1.0, 0.0).astype(jnp.bfloat16)
    before = jnp.dot(earlier, onehot.astype(jnp.bfloat16), preferred_element_type=jnp.float32) + running[...]
    lane_k = lax.broadcasted_iota(jnp.int32, (tm, nk), 1)
    out = jnp.zeros((tm, nk), jnp.float32)
    for k in range(nk):
        out = jnp.where(lane_k == k, jnp.sum(hots[k] * before, axis=-1, keepdims=True), out)
    rank_ref[...] = out.astype(jnp.int32)
    running[...] = running[...] + jnp.sum(onehot, axis=0, keepdims=True)
    count_ref[...] = running[...]


def expert_rank(idx):
    n, nk = idx.shape
    tm = RANK_TILE
    rank, counts = pl.pallas_call(
        _rank_body,
        grid=(n // tm,),
        in_specs=[pl.BlockSpec((tm, nk), lambda i: (i, 0))],
        out_specs=[pl.BlockSpec((tm, nk), lambda i: (i, 0)),
                   pl.BlockSpec((1, N_EXPERTS), lambda i: (0, 0))],
        out_shape=[jax.ShapeDtypeStruct((n, nk), jnp.int32),
                   jax.ShapeDtypeStruct((1, N_EXPERTS), jnp.float32)],
        scratch_shapes=[pltpu.VMEM((1, N_EXPERTS), jnp.float32)],
        compiler_params=pltpu.CompilerParams(dimension_semantics=("arbitrary",)),
        name="expert_rank",
    )(idx)
    return rank, counts[0].astype(jnp.int32)


def _dest_body(idx_ref, rank_ref, start_ref, dest_ref):
    tm, nk = idx_ref.shape
    idx = idx_ref[...]
    lane = lax.broadcasted_iota(jnp.int32, (tm, N_EXPERTS), 1)
    lane_k = lax.broadcasted_iota(jnp.int32, (tm, nk), 1)
    start = jnp.broadcast_to(start_ref[...], (tm, N_EXPERTS))
    out = jnp.zeros((tm, nk), jnp.float32)
    for k in range(nk):
        picked = jnp.sum(jnp.where(lane == idx[:, k:k + 1], start, 0.0), axis=-1, keepdims=True)
        out = jnp.where(lane_k == k, picked, out)
    dest_ref[...] = out.astype(jnp.int32) + rank_ref[...]


def expert_dest(idx, rank, pad_start):
    n, nk = idx.shape
    tm = RANK_TILE
    return pl.pallas_call(
        _dest_body,
        grid=(n // tm,),
        in_specs=[pl.BlockSpec((tm, nk), lambda i: (i, 0)), pl.BlockSpec((tm, nk), lambda i: (i, 0)),
                  pl.BlockSpec((1, N_EXPERTS), lambda i: (0, 0))],
        out_specs=pl.BlockSpec((tm, nk), lambda i: (i, 0)),
        out_shape=jax.ShapeDtypeStruct((n, nk), jnp.int32),
        compiler_params=pltpu.CompilerParams(dimension_semantics=("arbitrary",)),
        name="expert_dest",
    )(idx, rank, pad_start.astype(jnp.float32)[None])


EXPERT_TILE = 256


def _expert_body(be_ref, nu_ref, x_ref, wg_ref, wu_ref, wd_ref, y_ref):
    i = pl.program_id(0)

    @pl.when(i < nu_ref[0])
    def _():
        bf = jnp.bfloat16
        x = x_ref[...].astype(bf)
        a = jnp.dot(x, wg_ref[0].astype(bf), preferred_element_type=jnp.float32)
        u = jnp.dot(x, wu_ref[0].astype(bf), preferred_element_type=jnp.float32)
        hmid = (a / (1.0 + jnp.exp(-a))) * u
        y_ref[...] = jnp.dot(hmid.astype(bf), wd_ref[0].astype(bf), preferred_element_type=jnp.float32)

    @pl.when(i >= nu_ref[0])
    def _():
        y_ref[...] = jnp.zeros(y_ref.shape, jnp.float32)


def expert_ffn(xb, tile_expert, n_used, w_gate_e, w_up_e, w_down_e):
    rows, d = xb.shape
    tb = EXPERT_TILE
    de = w_gate_e.shape[2]
    grid_spec = pltpu.PrefetchScalarGridSpec(
        num_scalar_prefetch=2,
        grid=(rows // tb,),
        in_specs=[
            pl.BlockSpec((tb, d), lambda i, te, nu: (i, 0)),
            pl.BlockSpec((1, d, de), lambda i, te, nu: (te[i], 0, 0)),
            pl.BlockSpec((1, d, de), lambda i, te, nu: (te[i], 0, 0)),
            pl.BlockSpec((1, de, d), lambda i, te, nu: (te[i], 0, 0)),
        ],
        out_specs=pl.BlockSpec((tb, d), lambda i, te, nu: (i, 0)),
    )
    return pl.pallas_call(
        _expert_body,
        grid_spec=grid_spec,
        out_shape=jax.ShapeDtypeStruct((rows, d), jnp.float32),
        compiler_params=pltpu.CompilerParams(
            dimension_semantics=("arbitrary",), vmem_limit_bytes=48 * 1024 * 1024),
        name="expert_ffn",
    )(tile_expert, n_used, xb, w_gate_e, w_up_e, w_down_e)


def routed_experts(x2, idx, gate, w_gate_e, w_up_e, w_down_e):
    n, d = x2.shape
    a = n * TOP_K
    n_tiles = (a + N_EXPERTS * (EXPERT_TILE - 1) + EXPERT_TILE - 1) // EXPERT_TILE
    rows = n_tiles * EXPERT_TILE
    rank, counts = expert_rank(idx.astype(jnp.int32))
    padded = (counts + EXPERT_TILE - 1) // EXPERT_TILE * EXPERT_TILE
    pad_end = jnp.cumsum(padded)
    pad_start = pad_end - padded
    dest2 = expert_dest(idx.astype(jnp.int32), rank, pad_start)
    dest = dest2.reshape(-1)
    token = jnp.repeat(jnp.arange(n, dtype=jnp.int32), TOP_K)
    row_token = jnp.full((rows,), n, jnp.int32).at[dest].set(token)
    tile_start = jnp.arange(n_tiles, dtype=jnp.int32) * EXPERT_TILE
    tile_expert = jnp.minimum(jnp.sum(pad_end[None, :] <= tile_start[:, None], axis=1), N_EXPERTS - 1)
    n_used = (pad_end[-1:] // EXPERT_TILE).astype(jnp.int32)
    x_pad = jnp.concatenate([x2, jnp.zeros((1, d), x2.dtype)], axis=0)
    xb = x_pad[row_token]
    yb = expert_ffn(xb, tile_expert.astype(jnp.int32), n_used, w_gate_e, w_up_e, w_down_e)
    return (yb[dest2] * gate[:, :, None]).sum(axis=1)


def concat_heads(o_a, o_f):
    B, T = o_a.shape[:2]
    return jnp.concatenate([o_a.reshape(B * T, W_DSA), o_f.reshape(B * T, W_FOX)], axis=-1)


def finish_tokens(x2, heads, w_o, ln1_g, ln1_b, w_router, router_bias, w_gate_e, w_up_e, w_down_e,
                  w_gate_s, w_up_s, w_down_s, ln2_g, ln2_b):
    h, scores = mix_ln_router(heads, x2, w_o, ln1_g, ln1_b, w_router)
    idx, gate = route(scores, router_bias)
    routed = routed_experts(h, idx, gate, w_gate_e, w_up_e, w_down_e)
    return shared_ln_final(h, routed, w_gate_s, w_up_s, w_down_s, ln2_g, ln2_b)


def kernel(x_prompt, x_sample, cache_k_dsa, cache_v_dsa, cache_k_idx, cache_k_fox, cache_v_fox,
           cache_logf_fox, page_table, w_in, b_forget, idx_ln_g, idx_ln_b, w_o, ln1_g, ln1_b,
           w_router, router_bias, w_gate_e, w_up_e, w_down_e, w_gate_s, w_up_s, w_down_s,
           ln2_g, ln2_b):
    seq = x_prompt.shape[1]
    dec_seq = x_sample.shape[1]
    past = page_table.shape[1] * PAGE_SIZE
    pos_p = jnp.arange(seq, dtype=jnp.int32)
    pos_s = past + jnp.arange(dec_seq, dtype=jnp.int32)
    xp, xs = x_prompt, x_sample
    l = 0
    q_a, k_a, v_a, q_i, k_i, w_i, q_f, k_f, v_f, logf = project_mixers(
        xp, pos_p, w_in[l], b_forget[l], idx_ln_g[l], idx_ln_b[l])
    o_a = dsa_prompt(q_a, k_a, v_a, q_i, k_i, w_i)
    o_f = fox_prompt(q_f, k_f, v_f, logf)
    heads_p = concat_heads(o_a, o_f)
    outs_p = (k_a, v_a, k_i, k_f, v_f, logf.astype(x_prompt.dtype))
    q_a, k_a, v_a, q_i, k_i, w_i, q_f, k_f, v_f, logf = project_mixers(
        xs, pos_s, w_in[l], b_forget[l], idx_ln_g[l], idx_ln_b[l])
    o_a = dsa_sample(q_a, k_a, v_a, q_i, k_i, w_i, cache_k_dsa[l], cache_v_dsa[l], cache_k_idx[l], page_table)
    o_f = fox_sample(q_f, k_f, v_f, logf, cache_k_fox[l], cache_v_fox[l], cache_logf_fox[l], page_table)
    heads_s = concat_heads(o_a, o_f)
    outs_s = (k_a, v_a, k_i, k_f, v_f, logf.astype(x_sample.dtype))
    n_p = heads_p.shape[0]
    x2 = jnp.concatenate([xp.reshape(n_p, D_MODEL), xs.reshape(-1, D_MODEL)], axis=0)
    y = finish_tokens(x2, jnp.concatenate([heads_p, heads_s], axis=0), w_o[l], ln1_g[l], ln1_b[l],
                      w_router[l], router_bias[l], w_gate_e[l], w_up_e[l], w_down_e[l],
                      w_gate_s[l], w_up_s[l], w_down_s[l], ln2_g[l], ln2_b[l])
    yp = y[:n_p].reshape(x_prompt.shape)
    ys = y[n_p:].reshape(x_sample.shape)
    return (yp, ys) + tuple(o[None] for o in outs_p) + tuple(o[None] for o in outs_s)
```

```python
import functools
import jax
import jax.numpy as jnp
from jax import lax
import numpy as np
from jax.experimental import pallas as pl
from jax.experimental.pallas import tpu as pltpu

D_MODEL = 1024
PAGE_SIZE = 128
HEAD_DIM = 64
N_HEADS = D_MODEL // HEAD_DIM
H_DSA = N_HEADS // 2
H_FOX = N_HEADS - H_DSA
W_DSA = H_DSA * HEAD_DIM
W_FOX = H_FOX * HEAD_DIM
MIX_WIDTH = W_DSA + W_FOX
H_IDX = 8
D_IDX = 64
TOPK_MAX = 256
Q_BLOCK = 128
ROPE_THETA = 10000.0
N_EXPERTS = 256
TOP_K = 8
N_GROUPS = 8
TOPK_GROUPS = 4
ROUTED_SCALE = 2.5
LN_EPS = 1e-5
DEPTH = 1
DEEPNORM_ALPHA = (2 * DEPTH) ** 0.25

_IN_SIZES = (W_DSA, W_DSA, W_DSA, H_IDX * D_IDX, D_IDX, H_IDX, W_FOX, W_FOX, W_FOX, H_FOX)
IN_COLS = sum(_IN_SIZES)
SPLIT_POINTS = tuple(int(v) for v in np.cumsum(_IN_SIZES)[:-1])

LANES = 128


def _proj_body(x_ref, w_ref, o_ref):
    o_ref[...] = jnp.dot(x_ref[...].astype(jnp.bfloat16), w_ref[...],
                         preferred_element_type=jnp.float32)


def _project(x2, w):
    m, d = x2.shape
    c = w.shape[1]
    cp = (c + LANES - 1) // LANES * LANES
    wp = jnp.pad(w, ((0, 0), (0, cp - c))).astype(jnp.bfloat16)
    tm = 256
    out = pl.pallas_call(
        _proj_body,
        grid=(m // tm,),
        in_specs=[pl.BlockSpec((tm, d), lambda i: (i, 0)),
                  pl.BlockSpec((d, cp), lambda i: (0, 0))],
        out_specs=pl.BlockSpec((tm, cp), lambda i: (i, 0)),
        out_shape=jax.ShapeDtypeStruct((m, cp), jnp.float32),
        compiler_params=pltpu.CompilerParams(vmem_limit_bytes=48 * 1024 * 1024),
        name="in_proj",
    )(x2, wp)
    return out[:, :c]


def layer_norm(x, g, b):
    xf = x.astype(jnp.float32)
    mu = xf.mean(-1, keepdims=True)
    var = jnp.square(xf - mu).mean(-1, keepdims=True)
    return ((xf - mu) * lax.rsqrt(var + LN_EPS) * g.astype(jnp.float32) + b.astype(jnp.float32)).astype(x.dtype)


def rope(x, pos):
    half = x.shape[-1] // 2
    freqs = ROPE_THETA ** (-jnp.arange(half, dtype=jnp.float32) / half)
    ang = pos.astype(jnp.float32)[:, None] * freqs[None, :]
    cos = jnp.cos(ang)[None, :, None, :]
    sin = jnp.sin(ang)[None, :, None, :]
    xf = x.astype(jnp.float32)
    x1, x2 = xf[..., :half], xf[..., half:]
    return jnp.concatenate([x1 * cos - x2 * sin, x2 * cos + x1 * sin], axis=-1).astype(x.dtype)


def project_mixers(x, pos, w_in, b_forget, idx_ln_g, idx_ln_b):
    B, T, _ = x.shape
    h = _project(x.reshape(B * T, D_MODEL), w_in).reshape(B, T, IN_COLS)
    q_a, k_a, v_a, q_i, k_i, w_i, q_f, k_f, v_f, f_logit = jnp.split(h, SPLIT_POINTS, axis=-1)
    q_a = rope(q_a.reshape(B, T, H_DSA, HEAD_DIM), pos)
    k_a = rope(k_a.reshape(B, T, H_DSA, HEAD_DIM), pos)
    v_a = v_a.reshape(B, T, H_DSA, HEAD_DIM)
    q_i = rope(q_i.reshape(B, T, H_IDX, D_IDX), pos)
    k_i = rope(layer_norm(k_i, idx_ln_g, idx_ln_b)[:, :, None, :], pos)[:, :, 0, :]
    w_i = w_i.astype(jnp.float32) * (H_IDX ** -0.5 * D_IDX ** -0.5)
    q_f = q_f.reshape(B, T, H_FOX, HEAD_DIM)
    k_f = k_f.reshape(B, T, H_FOX, HEAD_DIM)
    v_f = v_f.reshape(B, T, H_FOX, HEAD_DIM)
    logf = jax.nn.log_sigmoid((f_logit + b_forget).astype(jnp.float32))
    return q_a, k_a, v_a, q_i, k_i, w_i, q_f, k_f, v_f, logf


INT_MIN = -2 ** 31
MASKED_SCORE = -1e30
RUNNING_MAX_INIT = -1e29
NT_DIMS = (((1,), (1,)), ((), ()))


def _head_major(x):
    return x.transpose(0, 2, 1, 3)


def _with_ones(v):
    return jnp.concatenate([v, jnp.ones_like(v)], axis=-1)


def _online_softmax_step(s, vc, m_ref, acc_ref):
    m_old = m_ref[...]
    m_new = jnp.maximum(m_old, jnp.max(s, axis=-1, keepdims=True))
    alpha = jnp.exp(m_old - m_new)
    p = jnp.exp(s - m_new)
    pv = jnp.einsum('hqk,hkd->hqd', p.astype(jnp.bfloat16), vc, preferred_element_type=jnp.float32)
    acc_ref[...] = alpha * acc_ref[...] + pv
    m_ref[...] = m_new


def _init_softmax_state(m_ref, acc_ref):
    m_ref[...] = jnp.full(m_ref.shape, RUNNING_MAX_INIT, jnp.float32)
    acc_ref[...] = jnp.zeros(acc_ref.shape, jnp.float32)


def _write_normalised(acc_ref, o_ref):
    a = acc_ref[...]
    o_ref[0] = a[:, :, :HEAD_DIM] / a[:, :, HEAD_DIM:]


def _kth_largest_key(count_ge, rows, kf, total, two_bits_per_step=False):
    c0 = count_ge(jnp.zeros((rows, 1), jnp.int32))
    nonneg = c0 >= kf
    prefix = jnp.where(nonneg, 0, INT_MIN).astype(jnp.int32)
    cnt = jnp.where(nonneg, c0, total)

    def bit_step(t, carry):
        prefix, cnt = carry
        cand = prefix + lax.shift_left(jnp.int32(1), 30 - t)
        cc = count_ge(cand)
        take = cc >= kf
        return jnp.where(take, cand, prefix), jnp.where(take, cc, cnt)

    def bit_pair_step(t, carry):
        prefix, cnt = carry
        lo = lax.shift_left(jnp.int32(1), 29 - 2 * t)
        hi = lax.shift_left(jnp.int32(1), 30 - 2 * t)
        c1, c2 = prefix + lo, prefix + hi
        c3 = c2 + lo
        n1, n2, n3 = count_ge(c1), count_ge(c2), count_ge(c3)
        t1, t2, t3 = n1 >= kf, n2 >= kf, n3 >= kf
        new_prefix = jnp.where(t3, c3, jnp.where(t2, c2, jnp.where(t1, c1, prefix)))
        new_cnt = jnp.where(t3, n3, jnp.where(t2, n2, jnp.where(t1, n1, cnt)))
        return new_prefix, new_cnt

    if two_bits_per_step:
        carry = lax.fori_loop(0, 15, bit_pair_step, (prefix, cnt))
        thr, cnt = bit_step(30, carry)
    else:
        thr, cnt = lax.fori_loop(0, 31, bit_step, (prefix, cnt))
    thr = jnp.maximum(thr, INT_MIN + 1)
    excess = cnt > kf
    any_excess = jnp.max(jnp.where(excess, 1.0, 0.0)) > 0.0
    return thr, excess, any_excess


def _last_kept_tie(count_ties_before, need, rows, index_bits):
    def idx_step(t, last):
        cand = last + lax.shift_left(jnp.int32(1), index_bits - 1 - t)
        return jnp.where(count_ties_before(cand) < need, cand, last)

    return lax.fori_loop(0, index_bits, idx_step, jnp.zeros((rows, 1), jnp.int32))


def _dsa_prompt_body(qi_ref, ki_ref, wi_ref, q_ref, k_ref, v_ref, o_ref,
                     keys, wb, m_s, acc_s, *, topk, tq, seq_bits):
    i = pl.program_id(1)
    n_chunks = i + 1
    row = lax.broadcasted_iota(jnp.int32, (tq, tq), 0)
    col = lax.broadcasted_iota(jnp.int32, (tq, tq), 1)
    kf = float(topk)

    w = wi_ref[0]
    for h in range(H_IDX):
        wb[h] = jnp.broadcast_to(w[:, h:h + 1], (tq, tq))
    qi_all = qi_ref[0].reshape(H_IDX * tq, D_IDX)

    def index_chunk(c, carry):
        kic = ki_ref[0, pl.ds(pl.multiple_of(c * tq, tq), tq), :]
        s = lax.dot_general(qi_all, kic, NT_DIMS, preferred_element_type=jnp.float32)
        sc = jnp.maximum(s[0:tq], 0.0) * wb[0]
        for h in range(1, H_IDX):
            sc = sc + jnp.maximum(s[h * tq:(h + 1) * tq], 0.0) * wb[h]
        sc = jnp.where(sc == 0.0, 0.0, sc)
        bits = pltpu.bitcast(sc, jnp.int32)
        keys[c] = jnp.where(bits < 0, bits ^ 0x7FFFFFFF, bits)
        return carry

    lax.fori_loop(0, n_chunks, index_chunk, 0)
    keys[i] = jnp.where(col <= row, keys[i], INT_MIN)

    def count_ge(cand):
        cb = jnp.broadcast_to(cand, (tq, tq))

        def body(c, acc):
            return acc + jnp.where(keys[c] >= cb, 1.0, 0.0)

        acc = lax.fori_loop(0, n_chunks, body, jnp.zeros((tq, tq), jnp.float32))
        return jnp.sum(acc, axis=-1, keepdims=True)

    thr, excess, any_excess = _kth_largest_key(count_ge, tq, kf, (n_chunks * tq).astype(jnp.float32))

    @pl.when(any_excess)
    def _():
        need = kf - count_ge(thr + 1)
        thb = jnp.broadcast_to(thr, (tq, tq))

        def count_ties_before(j):
            jb = jnp.broadcast_to(j, (tq, tq))

            def body(c, acc):
                hit = (keys[c] == thb) & ((col + c * tq) < jb)
                return acc + jnp.where(hit, 1.0, 0.0)

            acc = lax.fori_loop(0, n_chunks, body, jnp.zeros((tq, tq), jnp.float32))
            return jnp.sum(acc, axis=-1, keepdims=True)

        last = _last_kept_tie(count_ties_before, need, tq, seq_bits)
        lb = jnp.broadcast_to(last, (tq, tq))
        eb = jnp.broadcast_to(excess, (tq, tq))

        def drop(c, carry):
            kk = keys[c]
            keys[c] = jnp.where(eb & (kk == thb) & ((col + c * tq) > lb), INT_MIN, kk)
            return carry

        lax.fori_loop(0, n_chunks, drop, 0)

    thb = jnp.broadcast_to(thr, (tq, tq))
    _init_softmax_state(m_s, acc_s)
    qh = q_ref[0]

    def attend_chunk(c, carry):
        sel = keys[c] >= thb
        start = pl.multiple_of(c * tq, tq)
        kc = k_ref[0, :, pl.ds(start, tq), :]
        vc = v_ref[0, :, pl.ds(start, tq), :]
        s = jnp.einsum('hqd,hkd->hqk', qh, kc, preferred_element_type=jnp.float32)
        s = jnp.where(sel[None], s, MASKED_SCORE)
        _online_softmax_step(s, vc, m_s, acc_s)
        return carry

    lax.fori_loop(0, n_chunks, attend_chunk, 0)
    _write_normalised(acc_s, o_ref)


def dsa_prompt(q, k, v, q_i, k_i, w_i):
    B, S = q.shape[:2]
    topk = min(TOPK_MAX, S // 4)
    tq = Q_BLOCK
    nq = S // tq
    bf = jnp.bfloat16
    qh = _head_major(q * HEAD_DIM ** -0.5).astype(bf)
    kh = _head_major(k).astype(bf)
    vh = _with_ones(_head_major(v)).astype(bf)
    qih = _head_major(q_i).astype(bf)
    body = functools.partial(_dsa_prompt_body, topk=topk, tq=tq, seq_bits=max(1, (S - 1).bit_length()))
    o = pl.pallas_call(
        body,
        grid=(B, nq),
        in_specs=[
            pl.BlockSpec((1, H_IDX, tq, D_IDX), lambda b, i: (b, 0, i, 0)),
            pl.BlockSpec((1, S, D_IDX), lambda b, i: (b, 0, 0)),
            pl.BlockSpec((1, tq, H_IDX), lambda b, i: (b, i, 0)),
            pl.BlockSpec((1, H_DSA, tq, HEAD_DIM), lambda b, i: (b, 0, i, 0)),
            pl.BlockSpec((1, H_DSA, S, HEAD_DIM), lambda b, i: (b, 0, 0, 0)),
            pl.BlockSpec((1, H_DSA, S, 2 * HEAD_DIM), lambda b, i: (b, 0, 0, 0)),
        ],
        out_specs=pl.BlockSpec((1, H_DSA, tq, HEAD_DIM), lambda b, i: (b, 0, i, 0)),
        out_shape=jax.ShapeDtypeStruct((B, H_DSA, S, HEAD_DIM), jnp.float32),
        scratch_shapes=[
            pltpu.VMEM((nq, tq, tq), jnp.int32),
            pltpu.VMEM((H_IDX, tq, tq), jnp.float32),
            pltpu.VMEM((H_DSA, tq, tq), jnp.float32),
            pltpu.VMEM((H_DSA, tq, 2 * HEAD_DIM), jnp.float32),
        ],
        compiler_params=pltpu.CompilerParams(
            dimension_semantics=("arbitrary", "arbitrary"), vmem_limit_bytes=48 * 1024 * 1024),
        name="dsa_prompt",
    )(qih, k_i.astype(bf), w_i, qh, kh, vh)
    return _head_major(o)


def _score_keys(s, wcol, nt):
    sc = jnp.maximum(s[0:nt], 0.0) * wcol[0:nt]
    for h in range(1, H_IDX):
        sc = sc + jnp.maximum(s[h * nt:(h + 1) * nt], 0.0) * wcol[h * nt:(h + 1) * nt]
    sc = jnp.where(sc == 0.0, 0.0, sc)
    bits = pltpu.bitcast(sc, jnp.int32)
    return jnp.where(bits < 0, bits ^ 0x7FFFFFFF, bits)


def _idx_sample_body(pt_ref, qi_ref, wc_ref, kin_ref, *rest, g, n_pages, nt, topk, index_bits):
    ki_refs = rest[:g]
    pb_ref, sn_ref, keys = rest[g:]
    j = pl.program_id(1)
    bf = jnp.bfloat16
    slots = keys.shape[2]
    qi = qi_ref[0]
    wcol = wc_ref[0]
    for gi in range(g):
        s = jnp.dot(qi, ki_refs[gi][0].astype(bf), preferred_element_type=jnp.float32)
        keys[j * g + gi] = _score_keys(s, wcol, nt)

    @pl.when(j == pl.num_programs(1) - 1)
    def _():
        row = lax.broadcasted_iota(jnp.int32, (nt, slots), 0)
        col = lax.broadcasted_iota(jnp.int32, (nt, slots), 1)
        s = lax.dot_general(qi, kin_ref[0], NT_DIMS, preferred_element_type=jnp.float32)
        keys[n_pages] = jnp.where(col <= row, _score_keys(s, wcol, nt), INT_MIN)
        kf = float(topk)

        def count_ge(cand):
            hit = jnp.where(keys[...] >= cand[None], 1.0, 0.0)
            return jnp.sum(jnp.sum(hit, axis=0), axis=-1, keepdims=True)

        total = jnp.float32((n_pages + 1) * slots)
        thr, excess, any_excess = _kth_largest_key(count_ge, nt, kf, total, two_bits_per_step=True)

        @pl.when(any_excess)
        def _():
            need = kf - count_ge(thr + 1)
            pos = (lax.broadcasted_iota(jnp.int32, keys.shape, 0) * slots
                   + lax.broadcasted_iota(jnp.int32, keys.shape, 2))

            def count_ties_before(jj):
                hit = jnp.where((keys[...] == thr[None]) & (pos < jj[None]), 1.0, 0.0)
                return jnp.sum(jnp.sum(hit, axis=0), axis=-1, keepdims=True)

            last = _last_kept_tie(count_ties_before, need, nt, index_bits)
            kk = keys[...]
            keys[...] = jnp.where(excess[None] & (kk == thr[None]) & (pos > last[None]), INT_MIN, kk)

        bias = jnp.where(keys[...] >= thr[None], 0.0, MASKED_SCORE)
        sn_ref[0] = bias[n_pages]
        pb_ref[0] = bias[:n_pages]


def dsa_sample(q, k_new, v_new, q_i, k_i_new, w_i, cache_k, cache_v, cache_ki, page_table):
    B, T, H, D = q.shape
    n_pages = page_table.shape[1]
    slots = cache_ki.shape[1]
    L = n_pages * slots + T
    topk = min(TOPK_MAX, L // 4)
    g = IDX_PAGES_PER_STEP
    bf = jnp.bfloat16
    qi = _head_major(q_i).reshape(B, H_IDX * T, D_IDX).astype(bf)
    wcol = w_i.transpose(0, 2, 1).reshape(B, H_IDX * T, 1)
    ki_new = jnp.pad(k_i_new, ((0, 0), (0, slots - T), (0, 0))).astype(bf)
    cache_kit = _slots_minor(cache_ki)

    def page_map(gi):
        return lambda b, j, pt: (pt[b, j * g + gi], 0, 0)

    per_batch = lambda b, j, pt: (b, 0, 0)
    grid_spec = pltpu.PrefetchScalarGridSpec(
        num_scalar_prefetch=1,
        grid=(B, n_pages // g),
        in_specs=[
            pl.BlockSpec((1, H_IDX * T, D_IDX), per_batch),
            pl.BlockSpec((1, H_IDX * T, 1), per_batch),
            pl.BlockSpec((1, slots, D_IDX), per_batch),
        ] + [pl.BlockSpec((1, D_IDX, slots), page_map(gi)) for gi in range(g)],
        out_specs=[
            pl.BlockSpec((1, n_pages, T, slots), lambda b, j, pt: (b, 0, 0, 0)),
            pl.BlockSpec((1, T, slots), per_batch),
        ],
        scratch_shapes=[pltpu.VMEM((n_pages + 1, T, slots), jnp.int32)],
    )
    page_bias, bias_new = pl.pallas_call(
        functools.partial(_idx_sample_body, g=g, n_pages=n_pages, nt=T, topk=topk,
                          index_bits=max(1, ((n_pages + 1) * slots - 1).bit_length())),
        grid_spec=grid_spec,
        out_shape=[jax.ShapeDtypeStruct((B, n_pages, T, slots), jnp.float32),
                   jax.ShapeDtypeStruct((B, T, slots), jnp.float32)],
        compiler_params=pltpu.CompilerParams(
            dimension_semantics=("arbitrary", "arbitrary"), vmem_limit_bytes=48 * 1024 * 1024),
        name="idx_sample",
    )(page_table, qi, wcol, ki_new, *([cache_kit] * g))
    sel = bias_new[:, :, :T] == 0.0
    same = jnp.arange(H)[:, None] == jnp.arange(H)[None, :]
    nb = jnp.where(same[None, :, None, None, :] & sel[:, None, :, :, None], 0.0, MASKED_SCORE)
    new_bias = nb.reshape(B, H * T, T * H).astype(jnp.float32)
    row_bias = jnp.zeros((B, H * T, 1), jnp.float32)
    return paged_attention(q, k_new, v_new, _slots_minor(cache_k), _slots_minor(cache_v), page_table,
                           row_bias, page_bias, "token", new_bias)


def _fox_prompt_body(q_ref, k_ref, v_ref, ccol_ref, crow_ref, o_ref, cb, m_s, acc_s, *, tq):
    i = pl.program_id(1)
    row = lax.broadcasted_iota(jnp.int32, (tq, tq), 0)
    col = lax.broadcasted_iota(jnp.int32, (tq, tq), 1)
    ccol = ccol_ref[0]
    for h in range(H_FOX):
        cb[h] = jnp.broadcast_to(ccol[:, h:h + 1], (tq, tq))
    _init_softmax_state(m_s, acc_s)
    qh = q_ref[0]

    def attend_chunk(c, diagonal):
        start = pl.multiple_of(c * tq, tq)
        crow = crow_ref[0, :, pl.ds(start, tq)]
        kc = k_ref[0, :, pl.ds(start, tq), :]
        vc = v_ref[0, :, pl.ds(start, tq), :]
        s = jnp.einsum('hqd,hkd->hqk', qh, kc, preferred_element_type=jnp.float32)
        s = s + cb[...] - crow[:, None, :]
        if diagonal:
            s = jnp.where((col <= row)[None], s, MASKED_SCORE)
        _online_softmax_step(s, vc, m_s, acc_s)

    def full_chunk(c, carry):
        attend_chunk(c, False)
        return carry

    lax.fori_loop(0, i, full_chunk, 0)
    attend_chunk(i, True)
    _write_normalised(acc_s, o_ref)


def fox_prompt(q, k, v, logf):
    B, S = q.shape[:2]
    tq = Q_BLOCK
    bf = jnp.bfloat16
    c = jnp.cumsum(logf, axis=1)
    qh = _head_major(q * HEAD_DIM ** -0.5).astype(bf)
    kh = _head_major(k).astype(bf)
    vh = _with_ones(_head_major(v)).astype(bf)
    o = pl.pallas_call(
        functools.partial(_fox_prompt_body, tq=tq),
        grid=(B, S // tq),
        in_specs=[
            pl.BlockSpec((1, H_FOX, tq, HEAD_DIM), lambda b, i: (b, 0, i, 0)),
            pl.BlockSpec((1, H_FOX, S, HEAD_DIM), lambda b, i: (b, 0, 0, 0)),
            pl.BlockSpec((1, H_FOX, S, 2 * HEAD_DIM), lambda b, i: (b, 0, 0, 0)),
            pl.BlockSpec((1, tq, H_FOX), lambda b, i: (b, i, 0)),
            pl.BlockSpec((1, H_FOX, S), lambda b, i: (b, 0, 0)),
        ],
        out_specs=pl.BlockSpec((1, H_FOX, tq, HEAD_DIM), lambda b, i: (b, 0, i, 0)),
        out_shape=jax.ShapeDtypeStruct((B, H_FOX, S, HEAD_DIM), jnp.float32),
        scratch_shapes=[
            pltpu.VMEM((H_FOX, tq, tq), jnp.float32),
            pltpu.VMEM((H_FOX, tq, tq), jnp.float32),
            pltpu.VMEM((H_FOX, tq, 2 * HEAD_DIM), jnp.float32),
        ],
        compiler_params=pltpu.CompilerParams(
            dimension_semantics=("arbitrary", "arbitrary"), vmem_limit_bytes=48 * 1024 * 1024),
        name="fox_prompt",
    )(qh, kh, vh, c, c.transpose(0, 2, 1))
    return _head_major(o)


PAGES_PER_STEP = 16
IDX_PAGES_PER_STEP = 16


def _paged_attn_body(pt_ref, qbd_ref, q_ref, rb_ref, pb_ref, kn_ref, vn_ref, bn_ref, *rest, g, pb_axis, nh, nt):
    k_refs, v_refs = rest[:g], rest[g:2 * g]
    if pb_axis == "logf":
        lf_refs = rest[2 * g:3 * g]
        o_ref, m_s, l_s, acc_s, carry_s = rest[3 * g:]
    else:
        o_ref, m_s, l_s, acc_s = rest[2 * g:]
    j = pl.program_id(1)
    rows = nh * nt
    bf = jnp.bfloat16

    @pl.when(j == 0)
    def _():
        m_s[...] = jnp.full(m_s.shape, RUNNING_MAX_INIT, jnp.float32)
        l_s[...] = jnp.zeros(l_s.shape, jnp.float32)
        acc_s[...] = jnp.zeros(acc_s.shape, jnp.float32)
        if pb_axis == "logf":
            carry_s[...] = jnp.zeros(carry_s.shape, jnp.float32)

    head_bias = [None] * g
    if pb_axis == "logf":
        carry = carry_s[...]
        for gi in reversed(range(g)):
            lf = lf_refs[gi][0]
            slots = lf.shape[1]
            lane = lax.broadcasted_iota(jnp.int32, lf.shape, 1)
            incl = lf
            sh = 1
            while sh < slots:
                incl = incl + jnp.where(lane + sh < slots, pltpu.roll(incl, slots - sh, axis=1), 0.0)
                sh *= 2
            head_bias[gi] = carry + (incl - lf)
            carry = carry + incl[:, 0:1]
        carry_s[...] = carry

    qbd = qbd_ref[0]
    rb = rb_ref[0]
    logits = []
    for gi in range(g):
        kp = k_refs[gi][0]
        slots = kp.shape[2]
        k2 = kp.reshape(kp.shape[0] * kp.shape[1], slots).astype(bf)
        s = jnp.dot(qbd, k2, preferred_element_type=jnp.float32)
        if pb_axis == "logf":
            pb = jnp.broadcast_to(head_bias[gi][:, None, :], (nh, nt, slots)).reshape(rows, slots)
        elif pb_axis == "head":
            pb = jnp.broadcast_to(pb_ref[0, gi][:, None, :], (nh, nt, slots)).reshape(rows, slots)
        else:
            pb = jnp.broadcast_to(pb_ref[0, gi][None], (nh, nt, slots)).reshape(rows, slots)
        logits.append(s + rb + pb)
    m_old = m_s[...]
    m_new = m_old
    for s in logits:
        m_new = jnp.maximum(m_new, jnp.max(s, axis=-1, keepdims=True))
    alpha = jnp.exp(m_old - m_new)
    l_new = alpha * l_s[...]
    acc_new = alpha * acc_s[...]
    for gi in range(g):
        vp = v_refs[gi][0]
        v2 = vp.reshape(vp.shape[0] * vp.shape[1], vp.shape[2]).astype(bf)
        p = jnp.exp(logits[gi] - m_new)
        l_new = l_new + jnp.sum(p, axis=-1, keepdims=True)
        acc_new = acc_new + lax.dot_general(p.astype(bf), v2, NT_DIMS, preferred_element_type=jnp.float32)
    m_s[...] = m_new
    l_s[...] = l_new
    acc_s[...] = acc_new

    @pl.when(j == pl.num_programs(1) - 1)
    def _():
        s = lax.dot_general(q_ref[0], kn_ref[0], NT_DIMS, preferred_element_type=jnp.float32) + bn_ref[0]
        m_prev = m_s[...]
        m_fin = jnp.maximum(m_prev, jnp.max(s, axis=-1, keepdims=True))
        scale_prev = jnp.exp(m_prev - m_fin)
        p = jnp.exp(s - m_fin)
        l_fin = scale_prev * l_s[...] + jnp.sum(p, axis=-1, keepdims=True)
        acc_fin = scale_prev * acc_s[...] + jnp.dot(p.astype(bf), vn_ref[0], preferred_element_type=jnp.float32)
        o_ref[0] = acc_fin / l_fin


def _slots_minor(cache):
    return jnp.moveaxis(cache, 1, -1)


def _block_diag_heads(x):
    B, R, H, D = x.shape
    eye = jnp.eye(H, dtype=x.dtype)
    return (x[:, :, :, None, :] * eye[None, None, :, :, None]).reshape(B, R, H, H * D)


def paged_attention(q, k_new, v_new, cache_k, cache_v, page_table, row_bias, page_bias, pb_axis, new_bias):
    B, T, H, D = q.shape
    n_pages = page_table.shape[1]
    slots = cache_k.shape[3]
    g = PAGES_PER_STEP
    n_steps = n_pages // g
    from_logf = pb_axis == "logf"
    bf = jnp.bfloat16
    qs = _head_major(q) * HEAD_DIM ** -0.5
    qr = qs.reshape(B, H * T, D).astype(bf)
    qbd = _block_diag_heads(qs.transpose(0, 2, 1, 3)).transpose(0, 2, 1, 3).reshape(B, H * T, H * D).astype(bf)
    kn = k_new.reshape(B, T * H, D).astype(bf)
    vn = _block_diag_heads(v_new).reshape(B, T * H, H * D).astype(bf)

    def step_pages(j):
        return (n_steps - 1 - j) if from_logf else j

    def page_map(gi, rank):
        return lambda b, j, pt: (pt[b, step_pages(j) * g + gi],) + (0,) * (rank - 1)

    per_batch = lambda b, j, pt: (b, 0, 0)
    page_spec = [pl.BlockSpec((1, H, D, slots), page_map(gi, 4)) for gi in range(g)]
    scratch = [pltpu.VMEM((H * T, 1), jnp.float32),
               pltpu.VMEM((H * T, 1), jnp.float32),
               pltpu.VMEM((H * T, H * D), jnp.float32)]
    if from_logf:
        pb_operand = jnp.zeros((1, 1, 1, slots), jnp.float32)
        pb_spec = pl.BlockSpec((1, 1, 1, slots), lambda b, j, pt: (0, 0, 0, 0))
        extra_specs = [pl.BlockSpec((1, H, slots), page_map(gi, 3)) for gi in range(g)]
        extra_operands = [page_bias] * g
        scratch.append(pltpu.VMEM((H, 1), jnp.float32))
    else:
        pb_operand = page_bias
        pb_spec = pl.BlockSpec((1, g, page_bias.shape[2], slots), lambda b, j, pt: (b, j, 0, 0))
        extra_specs, extra_operands = [], []
    grid_spec = pltpu.PrefetchScalarGridSpec(
        num_scalar_prefetch=1,
        grid=(B, n_steps),
        in_specs=[
            pl.BlockSpec((1, H * T, H * D), per_batch),
            pl.BlockSpec((1, H * T, D), per_batch),
            pl.BlockSpec((1, H * T, 1), per_batch),
            pb_spec,
            pl.BlockSpec((1, T * H, D), per_batch),
            pl.BlockSpec((1, T * H, H * D), per_batch),
            pl.BlockSpec((1, H * T, T * H), per_batch),
        ] + page_spec + page_spec + extra_specs,
        out_specs=pl.BlockSpec((1, H * T, H * D), per_batch),
        scratch_shapes=scratch,
    )
    o = pl.pallas_call(
        functools.partial(_paged_attn_body, g=g, pb_axis=pb_axis, nh=H, nt=T),
        grid_spec=grid_spec,
        out_shape=jax.ShapeDtypeStruct((B, H * T, H * D), jnp.float32),
        compiler_params=pltpu.CompilerParams(
            dimension_semantics=("arbitrary", "arbitrary"), vmem_limit_bytes=48 * 1024 * 1024),
        name="paged_attention",
    )(page_table, qbd, qr, row_bias, pb_operand, kn, vn, new_bias,
      *([cache_k] * g), *([cache_v] * g), *extra_operands)
    o = o.reshape(B, H, T, H, D)
    hh = jnp.arange(H)
    return o[:, hh, :, hh, :].transpose(1, 2, 0, 3)


def fox_sample(q, k_new, v_new, logf_new, cache_k, cache_v, cache_logf, page_table):
    B, T = q.shape[:2]
    cn = jnp.cumsum(logf_new, axis=1)
    row_bias = cn.transpose(0, 2, 1).reshape(B, H_FOX * T, 1)
    d = cn.transpose(0, 2, 1)[:, :, :, None] - cn.transpose(0, 2, 1)[:, :, None, :]
    tpos = jnp.arange(T)
    d = jnp.where(tpos[None, :] <= tpos[:, None], d, MASKED_SCORE)
    same = jnp.arange(H_FOX)[:, None] == jnp.arange(H_FOX)[None, :]
    nb = jnp.where(same[None, :, None, None, :], d[:, :, :, :, None], MASKED_SCORE)
    new_bias = nb.reshape(B, H_FOX * T, T * H_FOX)
    return paged_attention(q, k_new, v_new, _slots_minor(cache_k), _slots_minor(cache_v), page_table,
                           row_bias, _slots_minor(cache_logf.astype(jnp.float32)), "logf", new_bias)


TOKEN_TILE = 512


def _layer_norm_rows(z, g, b):
    mu = jnp.mean(z, axis=-1, keepdims=True)
    zc = z - mu
    var = jnp.mean(zc * zc, axis=-1, keepdims=True)
    return zc * lax.rsqrt(var + LN_EPS) * g + b


def _mix_body(heads_ref, x_ref, wo_ref, g_ref, b_ref, wr_ref, h_ref, s_ref):
    bf = jnp.bfloat16
    mix = jnp.dot(heads_ref[...].astype(bf), wo_ref[...], preferred_element_type=jnp.float32)
    h = _layer_norm_rows(DEEPNORM_ALPHA * x_ref[...] + mix, g_ref[...], b_ref[...])
    h_ref[...] = h
    logits = lax.dot_general(wr_ref[...], h.astype(bf), NT_DIMS, preferred_element_type=jnp.float32)
    s_ref[...] = 1.0 / (1.0 + jnp.exp(-logits))


def mix_ln_router(heads, x2, w_o, ln1_g, ln1_b, w_router):
    n, d = x2.shape
    tm = TOKEN_TILE
    bf = jnp.bfloat16
    row = lambda i: (i, 0)
    whole = lambda i: (0, 0)
    return pl.pallas_call(
        _mix_body,
        grid=(n // tm,),
        in_specs=[pl.BlockSpec((tm, d), row), pl.BlockSpec((tm, d), row),
                  pl.BlockSpec((d, d), whole), pl.BlockSpec((1, d), whole), pl.BlockSpec((1, d), whole),
                  pl.BlockSpec((N_EXPERTS, d), whole)],
        out_specs=[pl.BlockSpec((tm, d), row), pl.BlockSpec((N_EXPERTS, tm), lambda i: (0, i))],
        out_shape=[jax.ShapeDtypeStruct((n, d), jnp.float32),
                   jax.ShapeDtypeStruct((N_EXPERTS, n), jnp.float32)],
        compiler_params=pltpu.CompilerParams(
            dimension_semantics=("arbitrary",), vmem_limit_bytes=48 * 1024 * 1024),
        name="mix_ln_router",
    )(heads, x2, w_o.astype(bf), ln1_g[None], ln1_b[None], w_router.T.astype(bf))


def _final_body(h_ref, r_ref, wg_ref, wu_ref, wd_ref, g_ref, b_ref, y_ref):
    bf = jnp.bfloat16
    h = h_ref[...]
    hb = h.astype(bf)
    a = jnp.dot(hb, wg_ref[...], preferred_element_type=jnp.float32)
    u = jnp.dot(hb, wu_ref[...], preferred_element_type=jnp.float32)
    shared = jnp.dot(((a / (1.0 + jnp.exp(-a))) * u).astype(bf), wd_ref[...], preferred_element_type=jnp.float32)
    y_ref[...] = _layer_norm_rows(DEEPNORM_ALPHA * h + (r_ref[...] + shared), g_ref[...], b_ref[...])


def shared_ln_final(h, routed, w_gate_s, w_up_s, w_down_s, ln2_g, ln2_b):
    n, d = h.shape
    tm = TOKEN_TILE
    ds = w_gate_s.shape[1]
    bf = jnp.bfloat16
    row = lambda i: (i, 0)
    whole = lambda i: (0, 0)
    return pl.pallas_call(
        _final_body,
        grid=(n // tm,),
        in_specs=[pl.BlockSpec((tm, d), row), pl.BlockSpec((tm, d), row),
                  pl.BlockSpec((d, ds), whole), pl.BlockSpec((d, ds), whole), pl.BlockSpec((ds, d), whole),
                  pl.BlockSpec((1, d), whole), pl.BlockSpec((1, d), whole)],
        out_specs=pl.BlockSpec((tm, d), row),
        out_shape=jax.ShapeDtypeStruct((n, d), jnp.float32),
        compiler_params=pltpu.CompilerParams(
            dimension_semantics=("arbitrary",), vmem_limit_bytes=48 * 1024 * 1024),
        name="shared_ln_final",
    )(h, routed, w_gate_s.astype(bf), w_up_s.astype(bf), w_down_s.astype(bf), ln2_g[None], ln2_b[None])


def _take_first_max(v, rowid):
    m = jnp.max(v, axis=0, keepdims=True)
    first = jnp.min(jnp.where(v == m, rowid, float(v.shape[0])), axis=0, keepdims=True)
    return m, first, jnp.where(rowid == first, -jnp.inf, v)


def _route_body(s_ref, b_ref, idx_ref, gate_ref):
    scores = s_ref[...]
    ne, tm = scores.shape
    gsz = ne // N_GROUPS
    biased = scores + b_ref[...]
    rid_g = lax.broadcasted_iota(jnp.int32, (gsz, tm), 0).astype(jnp.float32)
    gscore = []
    for g in range(N_GROUPS):
        m1, _, rest = _take_first_max(biased[g * gsz:(g + 1) * gsz], rid_g)
        gscore.append(m1 + jnp.max(rest, axis=0, keepdims=True))
    gs = jnp.concatenate(gscore, axis=0)
    gid = lax.broadcasted_iota(jnp.int32, (N_GROUPS, tm), 0).astype(jnp.float32)
    keep = jnp.zeros((N_GROUPS, tm), jnp.float32)
    for _ in range(TOPK_GROUPS):
        _, first, gs_next = _take_first_max(gs, gid)
        keep = jnp.where(gid == first, 1.0, keep)
        gs = gs_next
    masked = jnp.concatenate(
        [jnp.where(keep[g:g + 1] > 0.0, biased[g * gsz:(g + 1) * gsz], -jnp.inf) for g in range(N_GROUPS)], axis=0)
    rid = lax.broadcasted_iota(jnp.int32, (ne, tm), 0).astype(jnp.float32)
    picks, weights = [], []
    for _ in range(TOP_K):
        _, first, masked = _take_first_max(masked, rid)
        picks.append(first)
        weights.append(jnp.sum(jnp.where(rid == first, scores, 0.0), axis=0, keepdims=True))
    w = jnp.concatenate(weights, axis=0)
    idx_ref[...] = jnp.concatenate(picks, axis=0).astype(jnp.int32)
    gate_ref[...] = w / jnp.sum(w, axis=0, keepdims=True) * ROUTED_SCALE


def route(scores_t, router_bias):
    ne, n = scores_t.shape
    tm = TOKEN_TILE
    idx_t, gate_t = pl.pallas_call(
        _route_body,
        grid=(n // tm,),
        in_specs=[pl.BlockSpec((ne, tm), lambda i: (0, i)), pl.BlockSpec((ne, 1), lambda i: (0, 0))],
        out_specs=[pl.BlockSpec((TOP_K, tm), lambda i: (0, i)), pl.BlockSpec((TOP_K, tm), lambda i: (0, i))],
        out_shape=[jax.ShapeDtypeStruct((TOP_K, n), jnp.int32), jax.ShapeDtypeStruct((TOP_K, n), jnp.float32)],
        compiler_params=pltpu.CompilerParams(dimension_semantics=("arbitrary",)),
        name="route_topk",
    )(scores_t, router_bias.astype(jnp.float32)[:, None])
    return idx_t.T, gate_t.T


RANK_TILE = 256


def _rank_body(idx_ref, rank_ref, count_ref, running):
    i = pl.program_id(0)
    tm, nk = idx_ref.shape

    @pl.when(i == 0)
    def _():
        running[...] = jnp.zeros(running.shape, jnp.float32)

    idx = idx_ref[...]
    lane = lax.broadcasted_iota(jnp.int32, (tm, N_EXPERTS), 1)
    hots = [jnp.where(lane == idx[:, k:k + 1], 1.0, 0.0) for k in range(nk)]
    onehot = hots[0]
    for k in range(1, nk):
        onehot = onehot + hots[k]
    r = lax.broadcasted_iota(jnp.int32, (tm, tm), 0)
    c = lax.broadcasted_iota(jnp.int32, (tm, tm), 1)
    earlier = jnp.where(c < r, 1.0, 0.0).astype(jnp.bfloat16)
    before = jnp.dot(earlier, onehot.astype(jnp.bfloat16), preferred_element_type=jnp.float32) + running[...]
    lane_k = lax.broadcasted_iota(jnp.int32, (tm, nk), 1)
    out = jnp.zeros((tm, nk), jnp.float32)
    for k in range(nk):
        out = jnp.where(lane_k == k, jnp.sum(hots[k] * before, axis=-1, keepdims=True), out)
    rank_ref[...] = out.astype(jnp.int32)
    running[...] = running[...] + jnp.sum(onehot, axis=0, keepdims=True)
    count_ref[...] = running[...]


def expert_rank(idx):
    n, nk = idx.shape
    tm = RANK_TILE
    rank, counts = pl.pallas_call(
        _rank_body,
        grid=(n // tm,),
        in_specs=[pl.BlockSpec((tm, nk), lambda i: (i, 0))],
        out_specs=[pl.BlockSpec((tm, nk), lambda i: (i, 0)),
                   pl.BlockSpec((1, N_EXPERTS), lambda i: (0, 0))],
        out_shape=[jax.ShapeDtypeStruct((n, nk), jnp.int32),
                   jax.ShapeDtypeStruct((1, N_EXPERTS), jnp.float32)],
        scratch_shapes=[pltpu.VMEM((1, N_EXPERTS), jnp.float32)],
        compiler_params=pltpu.CompilerParams(dimension_semantics=("arbitrary",)),
        name="expert_rank",
    )(idx)
    return rank, counts[0].astype(jnp.int32)


def _dest_body(idx_ref, rank_ref, start_ref, dest_ref):
    tm, nk = idx_ref.shape
    idx = idx_ref[...]
    lane = lax.broadcasted_iota(jnp.int32, (tm, N_EXPERTS), 1)
    lane_k = lax.broadcasted_iota(jnp.int32, (tm, nk), 1)
    start = jnp.broadcast_to(start_ref[...], (tm, N_EXPERTS))
    out = jnp.zeros((tm, nk), jnp.float32)
    for k in range(nk):
        picked = jnp.sum(jnp.where(lane == idx[:, k:k + 1], start, 0.0), axis=-1, keepdims=True)
        out = jnp.where(lane_k == k, picked, out)
    dest_ref[...] = out.astype(jnp.int32) + rank_ref[...]


def expert_dest(idx, rank, pad_start):
    n, nk = idx.shape
    tm = RANK_TILE
    return pl.pallas_call(
        _dest_body,
        grid=(n // tm,),
        in_specs=[pl.BlockSpec((tm, nk), lambda i: (i, 0)), pl.BlockSpec((tm, nk), lambda i: (i, 0)),
                  pl.BlockSpec((1, N_EXPERTS), lambda i: (0, 0))],
        out_specs=pl.BlockSpec((tm, nk), lambda i: (i, 0)),
        out_shape=jax.ShapeDtypeStruct((n, nk), jnp.int32),
        compiler_params=pltpu.CompilerParams(dimension_semantics=("arbitrary",)),
        name="expert_dest",
    )(idx, rank, pad_start.astype(jnp.float32)[None])


EXPERT_TILE = 256


def _expert_body(be_ref, nu_ref, x_ref, wg_ref, wu_ref, wd_ref, y_ref):
    i = pl.program_id(0)

    @pl.when(i < nu_ref[0])
    def _():
        bf = jnp.bfloat16
        x = x_ref[...].astype(bf)
        a = jnp.dot(x, wg_ref[0].astype(bf), preferred_element_type=jnp.float32)
        u = jnp.dot(x, wu_ref[0].astype(bf), preferred_element_type=jnp.float32)
        hmid = (a / (1.0 + jnp.exp(-a))) * u
        y_ref[...] = jnp.dot(hmid.astype(bf), wd_ref[0].astype(bf), preferred_element_type=jnp.float32)

    @pl.when(i >= nu_ref[0])
    def _():
        y_ref[...] = jnp.zeros(y_ref.shape, jnp.float32)


def expert_ffn(xb, tile_expert, n_used, w_gate_e, w_up_e, w_down_e):
    rows, d = xb.shape
    tb = EXPERT_TILE
    de = w_gate_e.shape[2]
    grid_spec = pltpu.PrefetchScalarGridSpec(
        num_scalar_prefetch=2,
        grid=(rows // tb,),
        in_specs=[
            pl.BlockSpec((tb, d), lambda i, te, nu: (i, 0)),
            pl.BlockSpec((1, d, de), lambda i, te, nu: (te[i], 0, 0)),
            pl.BlockSpec((1, d, de), lambda i, te, nu: (te[i], 0, 0)),
            pl.BlockSpec((1, de, d), lambda i, te, nu: (te[i], 0, 0)),
        ],
        out_specs=pl.BlockSpec((tb, d), lambda i, te, nu: (i, 0)),
    )
    return pl.pallas_call(
        _expert_body,
        grid_spec=grid_spec,
        out_shape=jax.ShapeDtypeStruct((rows, d), jnp.float32),
        compiler_params=pltpu.CompilerParams(
            dimension_semantics=("arbitrary",), vmem_limit_bytes=48 * 1024 * 1024),
        name="expert_ffn",
    )(tile_expert, n_used, xb, w_gate_e, w_up_e, w_down_e)


def routed_experts(x2, idx, gate, w_gate_e, w_up_e, w_down_e):
    n, d = x2.shape
    a = n * TOP_K
    n_tiles = (a + N_EXPERTS * (EXPERT_TILE - 1) + EXPERT_TILE - 1) // EXPERT_TILE
    rows = n_tiles * EXPERT_TILE
    rank, counts = expert_rank(idx.astype(jnp.int32))
    padded = (counts + EXPERT_TILE - 1) // EXPERT_TILE * EXPERT_TILE
    pad_end = jnp.cumsum(padded)
    pad_start = pad_end - padded
    dest2 = expert_dest(idx.astype(jnp.int32), rank, pad_start)
    dest = dest2.reshape(-1)
    token = jnp.repeat(jnp.arange(n, dtype=jnp.int32), TOP_K)
    row_token = jnp.full((rows,), n, jnp.int32).at[dest].set(token)
    tile_start = jnp.arange(n_tiles, dtype=jnp.int32) * EXPERT_TILE
    tile_expert = jnp.minimum(jnp.sum(pad_end[None, :] <= tile_start[:, None], axis=1), N_EXPERTS - 1)
    n_used = (pad_end[-1:] // EXPERT_TILE).astype(jnp.int32)
    x_pad = jnp.concatenate([x2, jnp.zeros((1, d), x2.dtype)], axis=0)
    xb = x_pad[row_token]
    yb = expert_ffn(xb, tile_expert.astype(jnp.int32), n_used, w_gate_e, w_up_e, w_down_e)
    return (yb[dest2] * gate[:, :, None]).sum(axis=1)


def concat_heads(o_a, o_f):
    B, T = o_a.shape[:2]
    return jnp.concatenate([o_a.reshape(B * T, W_DSA), o_f.reshape(B * T, W_FOX)], axis=-1)


def finish_tokens(x2, heads, w_o, ln1_g, ln1_b, w_router, router_bias, w_gate_e, w_up_e, w_down_e,
                  w_gate_s, w_up_s, w_down_s, ln2_g, ln2_b):
    h, scores = mix_ln_router(heads, x2, w_o, ln1_g, ln1_b, w_router)
    idx, gate = route(scores, router_bias)
    routed = routed_experts(h, idx, gate, w_gate_e, w_up_e, w_down_e)
    return shared_ln_final(h, routed, w_gate_s, w_up_s, w_down_s, ln2_g, ln2_b)


def kernel(x_prompt, x_sample, cache_k_dsa, cache_v_dsa, cache_k_idx, cache_k_fox, cache_v_fox,
           cache_logf_fox, page_table, w_in, b_forget, idx_ln_g, idx_ln_b, w_o, ln1_g, ln1_b,
           w_router, router_bias, w_gate_e, w_up_e, w_down_e, w_gate_s, w_up_s, w_down_s,
           ln2_g, ln2_b):
    seq = x_prompt.shape[1]
    dec_seq = x_sample.shape[1]
    past = page_table.shape[1] * PAGE_SIZE
    pos_p = jnp.arange(seq, dtype=jnp.int32)
    pos_s = past + jnp.arange(dec_seq, dtype=jnp.int32)
    xp, xs = x_prompt, x_sample
    l = 0
    q_a, k_a, v_a, q_i, k_i, w_i, q_f, k_f, v_f, logf = project_mixers(
        xp, pos_p, w_in[l], b_forget[l], idx_ln_g[l], idx_ln_b[l])
    o_a = dsa_prompt(q_a, k_a, v_a, q_i, k_i, w_i)
    o_f = fox_prompt(q_f, k_f, v_f, logf)
    heads_p = concat_heads(o_a, o_f)
    outs_p = (k_a, v_a, k_i, k_f, v_f, logf.astype(x_prompt.dtype))
    q_a, k_a, v_a, q_i, k_i, w_i, q_f, k_f, v_f, logf = project_mixers(
        xs, pos_s, w_in[l], b_forget[l], idx_ln_g[l], idx_ln_b[l])
    o_a = dsa_sample(q_a, k_a, v_a, q_i, k_i, w_i, cache_k_dsa[l], cache_v_dsa[l], cache_k_idx[l], page_table)
    o_f = fox_sample(q_f, k_f, v_f, logf, cache_k_fox[l], cache_v_fox[l], cache_logf_fox[l], page_table)
    heads_s = concat_heads(o_a, o_f)
    outs_s = (k_a, v_a, k_i, k_f, v_f, logf.astype(x_sample.dtype))
    n_p = heads_p.shape[0]
    x2 = jnp.concatenate([xp.reshape(n_p, D_MODEL), xs.reshape(-1, D_MODEL)], axis=0)
    y = finish_tokens(x2, jnp.concatenate([heads_p, heads_s], axis=0), w_o[l], ln1_g[l], ln1_b[l],
                      w_router[l], router_bias[l], w_gate_e[l], w_up_e[l], w_down_e[l],
                      w_gate_s[l], w_up_s[l], w_down_s[l], ln2_g[l], ln2_b[l])
    yp = y[:n_p].reshape(x_prompt.shape)
    ys = y[n_p:].reshape(x_sample.shape)
    return (yp, ys) + tuple(o[None] for o in outs_p) + tuple(o[None] for o in outs_s)
```
